```python
import jax, jax.numpy as jnp
from jax import lax
import numpy as np

D_MODEL = 1024
BATCH = 8
SEQ = 2048
DEPTH = 1

N_META = 16
CHUNK = 128
EPS = 1e-6

M_HEADS = 4
M_DQK = 128
M_DV = 256
M_QK = M_HEADS * M_DQK
M_V = M_HEADS * M_DV
GATE_CAP = 15.0

S_HEADDIM = 64
S_INNER = D_MODEL
S_HEADS = S_INNER // S_HEADDIM
S_GROUPS = 4
S_HPG = S_HEADS // S_GROUPS
S_STATE = 128
S_CONV = 4
S_XBC = S_INNER + 2 * S_GROUPS * S_STATE

D_FF = -(-8 * D_MODEL // (3 * 256)) * 256

IN_SIZES = (M_QK, M_QK, M_V, M_V, M_HEADS, M_HEADS, S_INNER, S_XBC, S_HEADS, D_MODEL, D_MODEL)
IN_WIDTH = sum(IN_SIZES)

kernel_name = 'hybrid_mlstm_ssd_gated_block'


def rmsnorm(x, g):
    xf = x.astype(jnp.float32)
    y = xf * lax.rsqrt(jnp.mean(xf * xf, axis=-1, keepdims=True) + EPS)
    return (y * g.astype(jnp.float32)).astype(x.dtype)


def softcap(a):
    return GATE_CAP * jnp.tanh(a / GATE_CAP)


def _mlstm_state_step(carry, inp):
    C, n, m = carry
    b_tot, m_loc, S_loc, n_loc = inp
    m_new = jnp.maximum(b_tot + m, m_loc)
    a = jnp.exp(b_tot + m - m_new)
    s = jnp.exp(m_loc - m_new)
    C_new = a[..., None, None] * C + s[..., None, None] * S_loc
    n_new = a[..., None] * n + s[..., None] * n_loc
    return (C_new, n_new, m_new), (C, n, m)


def mlstm_chunked(q, k, v, i_pre, f_pre, valid):
    Bsz, T, H, Dk = q.shape
    Dv = v.shape[-1]
    nc = T // CHUNK
    f32 = jnp.float32
    vmask = valid[None, :, None]
    i_log = jnp.where(vmask, softcap(i_pre.astype(f32)), -jnp.inf)
    f_log = jnp.where(vmask, jax.nn.log_sigmoid(softcap(f_pre.astype(f32))), 0.0)
    q = (q * (Dk ** -0.5)).reshape(Bsz, nc, CHUNK, H, Dk)
    k = k.reshape(Bsz, nc, CHUNK, H, Dk)
    v = v.reshape(Bsz, nc, CHUNK, H, Dv)
    it = jnp.moveaxis(i_log.reshape(Bsz, nc, CHUNK, H), 2, -1)
    bt = jnp.cumsum(jnp.moveaxis(f_log.reshape(Bsz, nc, CHUNK, H), 2, -1), axis=-1)
    b_tot = bt[..., -1]
    causal = jnp.tril(jnp.ones((CHUNK, CHUNK), dtype=bool))
    d_log = jnp.where(causal, bt[..., :, None] - bt[..., None, :] + it[..., None, :], -jnp.inf)

    w_end = b_tot[..., None] - bt + it
    m_loc = jnp.max(w_end, axis=-1)
    wgt = jnp.exp(w_end - m_loc[..., None])
    vw = v * jnp.moveaxis(wgt, -1, 2)[..., None]
    S_loc = jnp.einsum('bcshv,bcshk->bchvk', vw, k)
    n_loc = jnp.einsum('bchs,bcshk->bchk', wgt, k)

    init = (jnp.zeros((Bsz, H, Dv, Dk), f32), jnp.zeros((Bsz, H, Dk), f32), jnp.zeros((Bsz, H), f32))
    xs = (jnp.moveaxis(b_tot, 1, 0), jnp.moveaxis(m_loc, 1, 0),
          jnp.moveaxis(S_loc.astype(f32), 1, 0), jnp.moveaxis(n_loc.astype(f32), 1, 0))
    _, (C_prev, n_prev, m_prev) = lax.scan(_mlstm_state_step, init, xs)
    C_prev = jnp.moveaxis(C_prev, 0, 1)
    n_prev = jnp.moveaxis(n_prev, 0, 1)
    m_prev = jnp.moveaxis(m_prev, 0, 1)

    inter_log = bt + m_prev[..., None]
    m_t = jnp.maximum(inter_log, jnp.max(d_log, axis=-1))
    qk = jnp.einsum('bcthk,bcshk->bchts', q, k)
    w_ts = jnp.exp(d_log - m_t[..., None]) * qk
    inter = jnp.exp(inter_log - m_t)
    num = (jnp.einsum('bchts,bcshv->bcthv', w_ts, v)
           + jnp.einsum('bchvk,bcthk->bcthv', C_prev, q) * jnp.moveaxis(inter, -1, 2)[..., None])
    den = jnp.sum(w_ts, axis=-1) + inter * jnp.einsum('bchk,bcthk->bcht', n_prev, q)
    denom = jnp.maximum(jnp.abs(den), jnp.exp(-m_t))
    h = num / jnp.moveaxis(denom, -1, 2)[..., None]
    return h.reshape(Bsz, T, H, Dv).astype(v.dtype)


def _ssd_state_step(S, inp):
    decay, st = inp
    return decay[..., None, None] * S + st, S


def ssd_chunked(x, dt, A, Bm, Cm):
    Bsz, T, G, J, P = x.shape
    N = Bm.shape[-1]
    nc = T // CHUNK
    x = x.reshape(Bsz, nc, CHUNK, G, J, P)
    dt = dt.reshape(Bsz, nc, CHUNK, G, J)
    Bm = Bm.reshape(Bsz, nc, CHUNK, G, N)
    Cm = Cm.reshape(Bsz, nc, CHUNK, G, N)
    cum = jnp.cumsum(dt * A, axis=2)
    ct = jnp.moveaxis(cum, 2, -1)
    dtT = jnp.moveaxis(dt, 2, -1)
    causal = jnp.tril(jnp.ones((CHUNK, CHUNK), dtype=bool))
    decay = jnp.exp(jnp.where(causal, ct[..., :, None] - ct[..., None, :], -jnp.inf))
    cb = jnp.einsum('bctgn,bcsgn->bcgts', Cm, Bm)
    w = cb[:, :, :, None] * decay * dtT[..., None, :]
    y_diag = jnp.einsum('bcgjts,bcsgjp->bctgjp', w, x)

    end_w = jnp.exp(ct[..., -1:] - ct) * dtT
    xw = x * jnp.moveaxis(end_w, -1, 2)[..., None]
    states = jnp.einsum('bcsgn,bcsgjp->bcgjpn', Bm, xw)
    chunk_decay = jnp.exp(ct[..., -1])
    init = jnp.zeros((Bsz, G, J, P, N), jnp.float32)
    _, S_prev = lax.scan(_ssd_state_step, init,
                         (jnp.moveaxis(chunk_decay, 1, 0), jnp.moveaxis(states.astype(jnp.float32), 1, 0)))
    S_prev = jnp.moveaxis(S_prev, 0, 1)
    y_off = jnp.einsum('bctgn,bcgjpn->bctgjp', Cm, S_prev) * jnp.exp(cum)[..., None]
    return (y_diag + y_off).reshape(Bsz, T, G, J, P)


def setup_inputs(seed: int = 0) -> dict:
    key = jax.random.key(seed)
    ks = jax.random.split(key, 24)
    f32 = jnp.float32
    nrm = lambda k, shape, scale: jax.random.normal(k, shape, f32) * scale
    dt0 = jnp.exp(jax.random.uniform(ks[10], (DEPTH, S_HEADS), f32) * (np.log(0.1) - np.log(1e-3)) + np.log(1e-3))
    return {
        'x': nrm(ks[0], (BATCH, SEQ, D_MODEL), 1.0),
        'meta': nrm(ks[1], (N_META, D_MODEL), 1.0),
        'norm1_g': 1.0 + nrm(ks[2], (DEPTH, D_MODEL), 0.05),
        'w_in': nrm(ks[3], (DEPTH, D_MODEL, IN_WIDTH), D_MODEL ** -0.5),
        'm_igate_b': nrm(ks[4], (DEPTH, M_HEADS), 0.1),
        'm_fgate_b': jnp.linspace(3.0, 6.0, M_HEADS, dtype=f32)[None] + nrm(ks[5], (DEPTH, M_HEADS), 0.1),
        'm_norm_g': 1.0 + nrm(ks[6], (DEPTH, M_HEADS, M_DV), 0.05),
        'm_proj': nrm(ks[7], (DEPTH, M_V, D_MODEL), M_V ** -0.5),
        's_conv_w': nrm(ks[8], (DEPTH, S_CONV, S_XBC), S_CONV ** -0.5),
        's_conv_b': nrm(ks[9], (DEPTH, S_XBC), 0.02),
        's_dt_bias': dt0 + jnp.log(-jnp.expm1(-dt0)),
        's_A_log': jnp.log(jax.random.uniform(ks[11], (DEPTH, S_HEADS), f32, 1.0, 16.0)),
        's_D': 1.0 + nrm(ks[12], (DEPTH, S_HEADS), 0.1),
        's_norm_g': 1.0 + nrm(ks[13], (DEPTH, S_INNER), 0.05),
        's_proj': nrm(ks[14], (DEPTH, S_INNER, D_MODEL), S_INNER ** -0.5),
        'w_out': nrm(ks[15], (DEPTH, D_MODEL, D_MODEL), D_MODEL ** -0.5),
        'norm2_g': 1.0 + nrm(ks[16], (DEPTH, D_MODEL), 0.05),
        'w_ffn_in': nrm(ks[17], (DEPTH, D_MODEL, 2 * D_FF), D_MODEL ** -0.5),
        'w_ffn_out': nrm(ks[18], (DEPTH, D_FF, D_MODEL), D_FF ** -0.5),
        'norm_f_g': 1.0 + nrm(ks[19], (D_MODEL,), 0.05),
    }


def reference(x, meta, norm1_g, w_in, m_igate_b, m_fgate_b, m_norm_g, m_proj, s_conv_w, s_conv_b,
              s_dt_bias, s_A_log, s_D, s_norm_g, s_proj, w_out, norm2_g, w_ffn_in, w_ffn_out, norm_f_g):
    Bsz, S, Dm = x.shape
    n_pad = CHUNK - N_META
    T = S + CHUNK
    h = jnp.concatenate([jnp.zeros((Bsz, n_pad, Dm), x.dtype),
                         jnp.broadcast_to(meta[None].astype(x.dtype), (Bsz, N_META, Dm)),
                         x], axis=1)
    valid = jnp.arange(T) >= n_pad
    split_at = tuple(int(i) for i in np.cumsum(IN_SIZES)[:-1])

    for l in range(DEPTH):
        u = rmsnorm(h, norm1_g[l])
        proj = u @ w_in[l]
        q, k, v, o_pre, i_pre, f_pre, z, xbc, dt_raw, ga, gb = jnp.split(proj, split_at, axis=-1)

        hm = mlstm_chunked(q.reshape(Bsz, T, M_HEADS, M_DQK), k.reshape(Bsz, T, M_HEADS, M_DQK),
                           v.reshape(Bsz, T, M_HEADS, M_DV), i_pre + m_igate_b[l], f_pre + m_fgate_b[l], valid)
        hm = rmsnorm(hm, m_norm_g[l]).reshape(Bsz, T, M_V) * jax.nn.sigmoid(o_pre)
        branch_a = hm @ m_proj[l]

        xbc = xbc * valid[None, :, None].astype(xbc.dtype)
        xpad = jnp.pad(xbc, ((0, 0), (S_CONV - 1, 0), (0, 0)))
        conv = s_conv_b[l] + sum(xpad[:, j:j + T] * s_conv_w[l, j] for j in range(S_CONV))
        xbc = jax.nn.silu(conv)
        xs, Bm, Cm = jnp.split(xbc, (S_INNER, S_INNER + S_GROUPS * S_STATE), axis=-1)
        xs = xs.reshape(Bsz, T, S_GROUPS, S_HPG, S_HEADDIM)
        dt = jax.nn.softplus(dt_raw.astype(jnp.float32) + s_dt_bias[l].astype(jnp.float32))
        dt = jnp.where(valid[None, :, None], dt, 0.0).reshape(Bsz, T, S_GROUPS, S_HPG)
        A = -jnp.exp(s_A_log[l].astype(jnp.float32)).reshape(S_GROUPS, S_HPG)
        ys = ssd_chunked(xs, dt, A, Bm.reshape(Bsz, T, S_GROUPS, S_STATE), Cm.reshape(Bsz, T, S_GROUPS, S_STATE))
        ys = ys + s_D[l].reshape(S_GROUPS, S_HPG)[..., None] * xs
        ys = (ys.reshape(Bsz, T, S_INNER) * jax.nn.silu(z)).astype(x.dtype)
        ys = rmsnorm(ys.reshape(Bsz, T, S_GROUPS, S_INNER // S_GROUPS),
                     s_norm_g[l].reshape(S_GROUPS, S_INNER // S_GROUPS)).reshape(Bsz, T, S_INNER)
        branch_b = ys @ s_proj[l]

        merged = jax.nn.sigmoid(ga) * branch_a + jax.nn.sigmoid(gb) * branch_b
        h = h + (merged @ w_out[l]).astype(h.dtype)

        u2 = rmsnorm(h, norm2_g[l])
        g_in, up = jnp.split(u2 @ w_ffn_in[l], 2, axis=-1)
        h = h + ((jax.nn.silu(g_in) * up) @ w_ffn_out[l]).astype(h.dtype)

    out = rmsnorm(h, norm_f_g)[:, CHUNK:]
    return out.astype(x.dtype)
```

```python
import functools

import jax
import jax.numpy as jnp
from jax import lax
from jax.experimental import pallas as pl
from jax.experimental.pallas import tpu as pltpu

F32 = jnp.float32
BF16 = jnp.bfloat16
HIGHEST = lax.Precision.HIGHEST

D_MODEL = 1024
N_META = 16
CHUNK = 128
EPS = 1e-6

M_HEADS = 4
M_DQK = 128
M_DV = 256
M_QK = M_HEADS * M_DQK
M_V = M_HEADS * M_DV
GATE_CAP = 15.0

S_HEADDIM = 64
S_INNER = D_MODEL
S_HEADS = S_INNER // S_HEADDIM
S_GROUPS = 4
S_HPG = S_HEADS // S_GROUPS
S_STATE = 128
S_CONV = 4
S_GW = S_HPG * S_HEADDIM
S_XBC = S_INNER + 2 * S_GROUPS * S_STATE

D_FF = 2816

Q0 = 0
K0 = Q0 + M_QK
V0 = K0 + M_QK
O0 = V0 + M_V
Z0 = O0 + M_V
X0 = Z0 + S_INNER
GA0 = X0 + S_XBC
GB0 = GA0 + D_MODEL
BIG_W = GB0 + D_MODEL
GATE_W = 128
GI0, GF0, GDT0, GEND = 0, M_HEADS, 2 * M_HEADS, 2 * M_HEADS + S_HEADS

LANE = 128
SUBLANE = 8
TM = 512
CPT = TM // CHUNK
NB = 512
FB = 256
VMEM_LIMIT = 56 * 1024 * 1024


def _softplus(x):
    return jnp.maximum(x, 0.0) + jnp.log1p(jnp.exp(-jnp.abs(x)))


def _const_spec(shape):
    nd = len(shape)
    return pl.BlockSpec(shape, lambda *_: (0,) * nd, pipeline_mode=pl.Buffered(1))


def _inproj_kernel(x_ref, g_ref, w_ref, big_ref, gate_ref, u_ref):
    xf = x_ref[...]
    ms = jnp.mean(xf * xf, axis=-1, keepdims=True)
    u_ref[...] = (xf * lax.rsqrt(ms + EPS) * g_ref[...]).astype(BF16)
    for n in range(BIG_W // NB):
        acc = jnp.dot(u_ref[...], w_ref[:, n * NB:(n + 1) * NB], preferred_element_type=F32)
        if (n + 1) * NB <= K0:
            acc = acc * (M_DQK ** -0.5)
        big_ref[:, n * NB:(n + 1) * NB] = acc.astype(BF16)
    gate_ref[...] = jnp.dot(u_ref[...], w_ref[:, BIG_W:BIG_W + GATE_W], preferred_element_type=F32)


def _inproj(rows, g, w_all, tm):
    n = rows.shape[0]
    return pl.pallas_call(
        _inproj_kernel,
        grid=(n // tm,),
        in_specs=[
            pl.BlockSpec((tm, D_MODEL), lambda i: (i, 0)),
            _const_spec((1, D_MODEL)),
            _const_spec((D_MODEL, BIG_W + GATE_W)),
        ],
        out_specs=[
            pl.BlockSpec((tm, BIG_W), lambda i: (i, 0)),
            pl.BlockSpec((tm, GATE_W), lambda i: (i, 0)),
        ],
        out_shape=[
            jax.ShapeDtypeStruct((n, BIG_W), BF16),
            jax.ShapeDtypeStruct((n, GATE_W), F32),
        ],
        scratch_shapes=[pltpu.VMEM((tm, D_MODEL), BF16)],
        compiler_params=pltpu.CompilerParams(
            dimension_semantics=("arbitrary",), vmem_limit_bytes=VMEM_LIMIT),
        name="inproj",
    )(rows, g, w_all)


def _chunk_step(r0, big_ref, gate_ref, p, ct_ref, n_ref, m_ref, st_ref, xh_ref, xc_ref,
                hm_ref, ys_ref, *, masked, emit):
    L = CHUNK
    rows = pl.ds(r0, L)
    lane = lax.broadcasted_iota(jnp.int32, (L, GATE_W), 1)
    row_i = lax.broadcasted_iota(jnp.int32, (L, L), 0)
    col_i = lax.broadcasted_iota(jnp.int32, (L, L), 1)
    causal = col_i <= row_i
    tril = causal.astype(F32)
    neg_inf = -jnp.inf

    pre = gate_ref[rows, :] + p["gbias"][...]
    sc = GATE_CAP * jnp.tanh(pre / GATE_CAP)
    i_log = sc
    f_log = -_softplus(-sc)
    dt = _softplus(pre)
    if masked:
        valid = lax.broadcasted_iota(jnp.int32, (L, GATE_W), 0) >= (L - N_META)
        i_log = jnp.where(valid, i_log, neg_inf)
        f_log = jnp.where(valid, f_log, 0.0)
        dt = jnp.where(valid, dt, 0.0)
    act = jnp.where(lane < GF0, i_log, jnp.where(lane < GDT0, f_log, jnp.where(lane < GEND, dt, 0.0)))
    a_row = -jnp.exp(p["alog"][...])
    cs = jnp.where(lane < GF0, 0.0, jnp.where(lane < GDT0, act, jnp.where(lane < GEND, act * a_row, 0.0)))
    cum = jnp.dot(tril, cs, precision=HIGHEST, preferred_element_type=F32)
    act_t = act.T
    cum_t = cum.T

    for h in range(M_HEADS):
        q = big_ref[rows, Q0 + h * M_DQK:Q0 + (h + 1) * M_DQK]
        k = big_ref[rows, K0 + h * M_DQK:K0 + (h + 1) * M_DQK]
        v = big_ref[rows, V0 + h * M_DV:V0 + (h + 1) * M_DV]
        gi, gf = GI0 + h, GF0 + h
        bt_c = cum[:, gf:gf + 1]
        bt_r = cum_t[gf:gf + 1, :]
        it_r = act_t[gi:gi + 1, :]
        it_c = act[:, gi:gi + 1]
        b_tot = cum[L - 1:L, gf:gf + 1]
        m_prev = m_ref[h:h + 1, 0:1]
        if emit:
            qk = lax.dot_general(q, k, (((1,), (1,)), ((), ())), preferred_element_type=F32)
            d_log = jnp.where(causal, bt_c - bt_r + it_r, neg_inf)
            mx = jnp.max(d_log, axis=1, keepdims=True)
            inter_log = bt_c + m_prev
            m_t = jnp.maximum(inter_log, mx)
            w_ts = jnp.exp(d_log - m_t) * qk
            inter = jnp.exp(inter_log - m_t)
            q_c = jnp.dot(q, ct_ref[h].astype(BF16), preferred_element_type=F32)
            num = jnp.dot(w_ts.astype(BF16), v, preferred_element_type=F32) + q_c * inter
            q_n = jnp.sum(q.astype(F32) * n_ref[h:h + 1, :], axis=1, keepdims=True)
            den = jnp.sum(w_ts, axis=1, keepdims=True) + inter * q_n
            denom = jnp.maximum(jnp.abs(den), jnp.exp(-m_t))
            hh = num / denom
            ms = jnp.mean(hh * hh, axis=1, keepdims=True)
            o_pre = big_ref[rows, O0 + h * M_DV:O0 + (h + 1) * M_DV].astype(F32)
            y = hh * lax.rsqrt(ms + EPS) * p["mng"][:, h * M_DV:(h + 1) * M_DV] * jax.nn.sigmoid(o_pre)
            hm_ref[rows, h * M_DV:(h + 1) * M_DV] = y.astype(BF16)
        w_end = b_tot - bt_c + it_c
        m_loc = jnp.max(w_end, axis=0, keepdims=True)
        wgt = jnp.exp(w_end - m_loc)
        vw = (v.astype(F32) * wgt).astype(BF16)
        s_loc = lax.dot_general(k, vw, (((0,), (0,)), ((), ())), preferred_element_type=F32)
        n_loc = jnp.sum(k.astype(F32) * wgt, axis=0, keepdims=True)
        m_new = jnp.maximum(b_tot + m_prev, m_loc)
        a_dec = jnp.exp(b_tot + m_prev - m_new)
        s_dec = jnp.exp(m_loc - m_new)
        ct_ref[h] = a_dec * ct_ref[h] + s_dec * s_loc
        n_ref[h:h + 1, :] = a_dec * n_ref[h:h + 1, :] + s_dec * n_loc
        m_ref[h:h + 1, :] = jnp.broadcast_to(m_new, (1, LANE))

    row8 = lax.broadcasted_iota(jnp.int32, (SUBLANE, FB), 0)
    for cb in range(S_XBC // FB):
        cols = slice(cb * FB, (cb + 1) * FB)
        xa = big_ref[rows, X0 + cb * FB:X0 + (cb + 1) * FB].astype(F32)
        if masked:
            xa = jnp.where(lax.broadcasted_iota(jnp.int32, (L, FB), 0) >= (L - N_META), xa, 0.0)
        hist = xh_ref[:, cols]
        acc = p["convb"][:, cols] + xa * p["convw"][S_CONV - 1:S_CONV, cols]
        for sh in range(1, S_CONV):
            rolled = pltpu.roll(xa, sh, axis=0)
            top = jnp.where(row8 < sh, pltpu.roll(hist, sh, axis=0), rolled[0:SUBLANE])
            shifted = jnp.concatenate([top, rolled[SUBLANE:]], axis=0)
            acc = acc + shifted * p["convw"][S_CONV - 1 - sh:S_CONV - sh, cols]
        xc_ref[:, cols] = acc * jax.nn.sigmoid(acc)
        xh_ref[:, cols] = xa[L - SUBLANE:L, :]

    lane_g = lax.broadcasted_iota(jnp.int32, (L, S_GW), 1)
    lane_g1 = lax.broadcasted_iota(jnp.int32, (1, S_GW), 1)
    for g in range(S_GROUPS):
        bg = xc_ref[:, S_INNER + g * S_STATE:S_INNER + (g + 1) * S_STATE].astype(BF16)
        cg = xc_ref[:, S_INNER + (S_GROUPS + g) * S_STATE:S_INNER + (S_GROUPS + g + 1) * S_STATE].astype(BF16)
        xg = xc_ref[:, g * S_GW:(g + 1) * S_GW]
        if emit:
            cbm = lax.dot_general(cg, bg, (((1,), (1,)), ((), ())), preferred_element_type=F32)
            y_diag = jnp.zeros((L, S_GW), F32)
            e_cum = jnp.zeros((L, S_GW), F32)
        e_end = jnp.zeros((L, S_GW), F32)
        e_tot = jnp.zeros((1, S_GW), F32)
        for j in range(S_HPG):
            gk = GDT0 + g * S_HPG + j
            in_head = (lane_g >= j * S_HEADDIM) & (lane_g < (j + 1) * S_HEADDIM)
            in_head1 = (lane_g1 >= j * S_HEADDIM) & (lane_g1 < (j + 1) * S_HEADDIM)
            ca_c = cum[:, gk:gk + 1]
            dt_c = act[:, gk:gk + 1]
            ca_tot = cum[L - 1:L, gk:gk + 1]
            if emit:
                ca_r = cum_t[gk:gk + 1, :]
                dt_r = act_t[gk:gk + 1, :]
                dec = jnp.exp(jnp.where(causal, ca_c - ca_r, neg_inf))
                w_m = (cbm * dec * dt_r).astype(BF16)
                x_m = jnp.where(in_head, xg, 0.0).astype(BF16)
                y_diag = y_diag + jnp.dot(w_m, x_m, preferred_element_type=F32)
                e_cum = jnp.where(in_head, jnp.exp(ca_c), e_cum)
            e_end = jnp.where(in_head, jnp.exp(ca_tot - ca_c) * dt_c, e_end)
            e_tot = jnp.where(in_head1, jnp.exp(ca_tot), e_tot)
        if emit:
            y_off = jnp.dot(cg, st_ref[g].astype(BF16), preferred_element_type=F32) * e_cum
            y = y_diag + y_off + p["sd"][:, g * S_GW:(g + 1) * S_GW] * xg
            z = big_ref[rows, Z0 + g * S_GW:Z0 + (g + 1) * S_GW].astype(F32)
            yz = y * (z * jax.nn.sigmoid(z))
            ms = jnp.mean(yz * yz, axis=1, keepdims=True)
            ys_ref[rows, g * S_GW:(g + 1) * S_GW] = (
                yz * lax.rsqrt(ms + EPS) * p["sng"][:, g * S_GW:(g + 1) * S_GW]).astype(BF16)
        xw = (xg * e_end).astype(BF16)
        s_new = lax.dot_general(bg, xw, (((0,), (0,)), ((), ())), preferred_element_type=F32)
        st_ref[g] = st_ref[g] * e_tot + s_new


_SMALL_KEYS = ("gbias", "alog", "mng", "sd", "sng", "convw", "convb")


def _init_kernel(big_ref, gate_ref, *refs):
    p = dict(zip(_SMALL_KEYS, refs[:len(_SMALL_KEYS)]))
    ct_ref, n_ref, m_ref, st_ref, xh_ref, xc_ref = refs[len(_SMALL_KEYS):]
    ct_ref[...] = jnp.zeros_like(ct_ref)
    n_ref[...] = jnp.zeros_like(n_ref)
    m_ref[...] = jnp.zeros_like(m_ref)
    st_ref[...] = jnp.zeros_like(st_ref)
    xh_ref[...] = jnp.zeros_like(xh_ref)
    _chunk_step(0, big_ref, gate_ref, p, ct_ref, n_ref, m_ref, st_ref, xh_ref, xc_ref,
                None, None, masked=True, emit=False)


_STATE_SHAPES = (
    (M_HEADS, M_DQK, M_DV),
    (SUBLANE, M_DQK),
    (SUBLANE, LANE),
    (S_GROUPS, S_STATE, S_GW),
    (SUBLANE, S_XBC),
)


def _init_state(big_m, gate_m, small):
    return pl.pallas_call(
        _init_kernel,
        grid=(1,),
        in_specs=[_const_spec(big_m.shape), _const_spec(gate_m.shape)] + [_const_spec(s.shape) for s in small],
        out_specs=[_const_spec(s) for s in _STATE_SHAPES],
        out_shape=[jax.ShapeDtypeStruct(s, F32) for s in _STATE_SHAPES],
        scratch_shapes=[pltpu.VMEM((CHUNK, S_XBC), F32)],
        compiler_params=pltpu.CompilerParams(
            dimension_semantics=("arbitrary",), vmem_limit_bytes=VMEM_LIMIT),
        name="init_state",
    )(big_m, gate_m, *small)


def _mixer_kernel(big_ref, gate_ref, x_ref, *refs):
    ns = len(_SMALL_KEYS)
    p = dict(zip(_SMALL_KEYS, refs[:ns]))
    ct0, n0, m0, st0, xh0 = refs[ns:ns + 5]
    mproj_ref, sproj_ref, wout_ref = refs[ns + 5:ns + 8]
    h1_ref = refs[ns + 8]
    ct_ref, n_ref, m_ref, st_ref, xh_ref, xc_ref, hm_ref, ys_ref, mg_ref = refs[ns + 9:]

    @pl.when(pl.program_id(1) == 0)
    def _():
        ct_ref[...] = ct0[...]
        n_ref[...] = n0[...]
        m_ref[...] = m0[...]
        st_ref[...] = st0[...]
        xh_ref[...] = xh0[...]

    def body(c, carry):
        r0 = pl.multiple_of(c * CHUNK, CHUNK)
        _chunk_step(r0, big_ref, gate_ref, p, ct_ref, n_ref, m_ref, st_ref, xh_ref, xc_ref,
                    hm_ref, ys_ref, masked=False, emit=True)
        return carry

    lax.fori_loop(0, CPT, body, 0)

    for nb in range(D_MODEL // FB):
        cols = slice(nb * FB, (nb + 1) * FB)
        br_a = jnp.dot(hm_ref[...], mproj_ref[:, cols], preferred_element_type=F32)
        br_b = jnp.dot(ys_ref[...], sproj_ref[:, cols], preferred_element_type=F32)
        ga = big_ref[:, GA0 + nb * FB:GA0 + (nb + 1) * FB].astype(F32)
        gb = big_ref[:, GB0 + nb * FB:GB0 + (nb + 1) * FB].astype(F32)
        mg_ref[:, cols] = (jax.nn.sigmoid(ga) * br_a + jax.nn.sigmoid(gb) * br_b).astype(BF16)
    for nb in range(D_MODEL // FB):
        cols = slice(nb * FB, (nb + 1) * FB)
        h1_ref[:, cols] = x_ref[:, cols] + jnp.dot(mg_ref[...], wout_ref[:, cols], preferred_element_type=F32)


def _mixer(big, gate, x2, small, init, mproj, sproj, wout, batch):
    n = x2.shape[0]
    tiles = n // batch // TM
    row_map = lambda b, j: (b * tiles + j, 0)
    return pl.pallas_call(
        _mixer_kernel,
        grid=(batch, tiles),
        in_specs=[
            pl.BlockSpec((TM, BIG_W), row_map),
            pl.BlockSpec((TM, GATE_W), row_map),
            pl.BlockSpec((TM, D_MODEL), row_map),
        ] + [_const_spec(s.shape) for s in small]
          + [_const_spec(s) for s in _STATE_SHAPES]
          + [_const_spec((D_MODEL, D_MODEL))] * 3,
        out_specs=pl.BlockSpec((TM, D_MODEL), row_map),
        out_shape=jax.ShapeDtypeStruct((n, D_MODEL), F32),
        scratch_shapes=[pltpu.VMEM(s, F32) for s in _STATE_SHAPES] + [
            pltpu.VMEM((CHUNK, S_XBC), F32),
            pltpu.VMEM((TM, M_V), BF16),
            pltpu.VMEM((TM, S_INNER), BF16),
            pltpu.VMEM((TM, D_MODEL), BF16),
        ],
        compiler_params=pltpu.CompilerParams(
            dimension_semantics=("arbitrary", "arbitrary"), vmem_limit_bytes=VMEM_LIMIT),
        name="mixer",
    )(big, gate, x2, *small, *init, mproj, sproj, wout)


def _ffn_kernel(h_ref, g2_ref, gf_ref, w1_ref, w2_ref, o_ref, u_ref, hid_ref):
    h = h_ref[...]
    ms = jnp.mean(h * h, axis=-1, keepdims=True)
    u_ref[...] = (h * lax.rsqrt(ms + EPS) * g2_ref[...]).astype(BF16)
    for jb in range(D_FF // FB):
        gate = jnp.dot(u_ref[...], w1_ref[:, jb * FB:(jb + 1) * FB], preferred_element_type=F32)
        up = jnp.dot(u_ref[...], w1_ref[:, D_FF + jb * FB:D_FF + (jb + 1) * FB], preferred_element_type=F32)
        hid_ref[:, jb * FB:(jb + 1) * FB] = (gate * jax.nn.sigmoid(gate) * up).astype(BF16)
    for nb in range(D_MODEL // FB):
        cols = slice(nb * FB, (nb + 1) * FB)
        o_ref[:, cols] = h_ref[:, cols] + jnp.dot(hid_ref[...], w2_ref[:, cols], preferred_element_type=F32)
    h2 = o_ref[...]
    ms2 = jnp.mean(h2 * h2, axis=-1, keepdims=True)
    o_ref[...] = h2 * lax.rsqrt(ms2 + EPS) * gf_ref[...]


def _ffn(h1, g2, gf, w1, w2):
    n = h1.shape[0]
    return pl.pallas_call(
        _ffn_kernel,
        grid=(n // TM,),
        in_specs=[
            pl.BlockSpec((TM, D_MODEL), lambda i: (i, 0)),
            _const_spec((1, D_MODEL)),
            _const_spec((1, D_MODEL)),
            _const_spec((D_MODEL, 2 * D_FF)),
            _const_spec((D_FF, D_MODEL)),
        ],
        out_specs=pl.BlockSpec((TM, D_MODEL), lambda i: (i, 0)),
        out_shape=jax.ShapeDtypeStruct((n, D_MODEL), F32),
        scratch_shapes=[pltpu.VMEM((TM, D_MODEL), BF16), pltpu.VMEM((TM, D_FF), BF16)],
        compiler_params=pltpu.CompilerParams(
            dimension_semantics=("arbitrary",), vmem_limit_bytes=VMEM_LIMIT),
        name="ffn",
    )(h1, g2, gf, w1, w2)


def _pad_lanes(row, width):
    return jnp.pad(row, ((0, 0), (0, width - row.shape[1])))


def kernel(x, meta, norm1_g, w_in, m_igate_b, m_fgate_b, m_norm_g, m_proj, s_conv_w, s_conv_b,
           s_dt_bias, s_A_log, s_D, s_norm_g, s_proj, w_out, norm2_g, w_ffn_in, w_ffn_out, norm_f_g):
    bsz, seq, dm = x.shape
    assert dm == D_MODEL and seq % TM == 0 and w_in.shape[0] == 1
    sizes = (M_QK, M_QK, M_V, M_V, M_HEADS, M_HEADS, S_INNER, S_XBC, S_HEADS, D_MODEL, D_MODEL)
    offs = [0]
    for s in sizes:
        offs.append(offs[-1] + s)
    wq, wk, wv, wo, wi, wf, wz, wx, wdt, wga, wgb = (w_in[0][:, offs[i]:offs[i + 1]] for i in range(len(sizes)))
    w_all = jnp.concatenate(
        [wq, wk, wv, wo, wz, wx, wga, wgb, wi, wf, wdt,
         jnp.zeros((D_MODEL, GATE_W - GEND), w_in.dtype)], axis=1).astype(BF16)

    small = (
        _pad_lanes(jnp.concatenate([m_igate_b[0], m_fgate_b[0], s_dt_bias[0]])[None].astype(F32), GATE_W),
        _pad_lanes(jnp.concatenate([jnp.zeros((GDT0,), F32), s_A_log[0].astype(F32)])[None], GATE_W),
        m_norm_g[0].reshape(1, M_V).astype(F32),
        jnp.repeat(s_D[0].astype(F32), S_HEADDIM)[None],
        s_norm_g[0].reshape(1, S_INNER).astype(F32),
        jnp.pad(s_conv_w[0].astype(F32), ((0, SUBLANE - S_CONV), (0, 0))),
        s_conv_b[0][None].astype(F32),
    )
    g1 = norm1_g[0][None].astype(F32)

    meta_rows = jnp.concatenate([jnp.zeros((CHUNK - N_META, D_MODEL), x.dtype), meta.astype(x.dtype)], axis=0)
    big_m, gate_m = _inproj(meta_rows, g1, w_all, CHUNK)
    init = _init_state(big_m, gate_m, small)

    x2 = x.reshape(bsz * seq, dm)
    big, gate = _inproj(x2, g1, w_all, TM)
    h1 = _mixer(big, gate, x2, small, init, m_proj[0].astype(BF16), s_proj[0].astype(BF16),
                w_out[0].astype(BF16), bsz)
    out = _ffn(h1, norm2_g[0][None].astype(F32), norm_f_g[None].astype(F32),
               w_ffn_in[0].astype(BF16), w_ffn_out[0].astype(BF16))
    return out.reshape(bsz, seq, dm)
```

```python
import functools
import math

import jax
import jax.numpy as jnp
from jax import lax
from jax.experimental import pallas as pl
from jax.experimental.pallas import tpu as pltpu

F32 = jnp.float32
BF16 = jnp.bfloat16
HIGHEST = lax.Precision.HIGHEST
LOG2E = math.log2(math.e)

D_MODEL = 1024
N_META = 16
CHUNK = 128
EPS = 1e-6

M_HEADS = 4
M_DQK = 128
M_DV = 256
M_QK = M_HEADS * M_DQK
M_V = M_HEADS * M_DV
GATE_CAP = 15.0

S_HEADDIM = 64
S_INNER = D_MODEL
S_HEADS = S_INNER // S_HEADDIM
S_GROUPS = 4
S_HPG = S_HEADS // S_GROUPS
S_STATE = 128
S_CONV = 4
S_GW = S_HPG * S_HEADDIM
S_XBC = S_INNER + 2 * S_GROUPS * S_STATE
XB0 = S_INNER
XC0 = S_INNER + S_GROUPS * S_STATE

D_FF = 2816

Q0 = 0
K0 = Q0 + M_QK
V0 = K0 + M_QK
O0 = V0 + M_V
Z0 = O0 + M_V
X0 = Z0 + S_INNER
GA0 = X0 + S_XBC
GB0 = GA0 + D_MODEL
BIG_W = GB0 + D_MODEL
GATE_W = 128
GI0, GF0, GDT0, GEND = 0, M_HEADS, 2 * M_HEADS, 2 * M_HEADS + S_HEADS

LANE = 128
SUBLANE = 8
TM = 512
CPT = TM // CHUNK
NB = 512
FB = 256
VMEM_LIMIT = 56 * 1024 * 1024


def _softplus(x):
    return jnp.maximum(x, 0.0) + jnp.log1p(jnp.exp(-jnp.abs(x)))


def _const_spec(shape):
    nd = len(shape)
    return pl.BlockSpec(shape, lambda *_: (0,) * nd, pipeline_mode=pl.Buffered(1))


def _conv_silu(xa, hist, cw_ref, cb_ref, cols):
    row8 = lax.broadcasted_iota(jnp.int32, (SUBLANE, xa.shape[1]), 0)
    acc = cb_ref[:, cols] + xa * cw_ref[S_CONV - 1:S_CONV, cols]
    for sh in range(1, S_CONV):
        rolled = pltpu.roll(xa, sh, axis=0)
        top = jnp.where(row8 < sh, pltpu.roll(hist, sh, axis=0), rolled[0:SUBLANE])
        shifted = jnp.concatenate([top, rolled[SUBLANE:]], axis=0)
        acc = acc + shifted * cw_ref[S_CONV - 1 - sh:S_CONV - sh, cols]
    return acc * jax.nn.sigmoid(acc)


def _inproj_kernel(x_ref, g_ref, w_ref, *refs, activate):
    if activate:
        xh0_ref, cw_ref, cb_ref, big_ref, gate_ref, u_ref, xh_ref = refs

        @pl.when(pl.program_id(1) == 0)
        def _():
            xh_ref[...] = xh0_ref[...]
    else:
        big_ref, gate_ref, u_ref = refs
    tm = x_ref.shape[0]
    xf = x_ref[...]
    ms = jnp.mean(xf * xf, axis=-1, keepdims=True)
    u_ref[...] = (xf * lax.rsqrt(ms + EPS) * g_ref[...]).astype(BF16)
    for n in range(BIG_W // NB):
        c0 = n * NB
        acc = jnp.dot(u_ref[...], w_ref[:, c0:c0 + NB], preferred_element_type=F32)
        if c0 < K0:
            acc = acc * (M_DQK ** -0.5)
        if activate:
            if O0 <= c0 < Z0 or c0 >= GA0:
                acc = jax.nn.sigmoid(acc)
            elif Z0 <= c0 < X0:
                acc = acc * jax.nn.sigmoid(acc)
            elif X0 <= c0 < GA0:
                cols = slice(c0 - X0, c0 - X0 + NB)
                raw = acc
                acc = _conv_silu(raw, xh_ref[:, cols], cw_ref, cb_ref, cols)
                xh_ref[:, cols] = raw[tm - SUBLANE:tm, :]
        big_ref[:, c0:c0 + NB] = acc.astype(BF16)
    gate_ref[...] = jnp.dot(u_ref[...], w_ref[:, BIG_W:BIG_W + GATE_W], preferred_element_type=F32)


def _inproj(rows, g, w_all, tm, conv=None, batch=1):
    n = rows.shape[0]
    tiles = n // batch // tm
    row_map = lambda b, j: (b * tiles + j, 0)
    activate = conv is not None
    extra = list(conv) if activate else []
    scratch = [pltpu.VMEM((tm, D_MODEL), BF16)]
    if activate:
        scratch.append(pltpu.VMEM((SUBLANE, S_XBC), F32))
    return pl.pallas_call(
        functools.partial(_inproj_kernel, activate=activate),
        grid=(batch, tiles),
        in_specs=[
            pl.BlockSpec((tm, D_MODEL), row_map),
            _const_spec((1, D_MODEL)),
            _const_spec((D_MODEL, BIG_W + GATE_W)),
        ] + [_const_spec(a.shape) for a in extra],
        out_specs=[
            pl.BlockSpec((tm, BIG_W), row_map),
            pl.BlockSpec((tm, GATE_W), row_map),
        ],
        out_shape=[
            jax.ShapeDtypeStruct((n, BIG_W), BF16),
            jax.ShapeDtypeStruct((n, GATE_W), F32),
        ],
        scratch_shapes=scratch,
        compiler_params=pltpu.CompilerParams(
            dimension_semantics=("arbitrary", "arbitrary"), vmem_limit_bytes=VMEM_LIMIT),
        name="inproj_act" if activate else "inproj_raw",
    )(rows, g, w_all, *extra)


def _gate_block(pre, alog_row, masked):
    L = pre.shape[0]
    lane = lax.broadcasted_iota(jnp.int32, (L, GATE_W), 1)
    sc = GATE_CAP * jnp.tanh(pre / GATE_CAP)
    i_log = sc * LOG2E
    f_log = -_softplus(-sc) * LOG2E
    dt = _softplus(pre)
    if masked:
        valid = lax.broadcasted_iota(jnp.int32, (L, GATE_W), 0) >= (L - N_META)
        i_log = jnp.where(valid, i_log, -jnp.inf)
        f_log = jnp.where(valid, f_log, 0.0)
        dt = jnp.where(valid, dt, 0.0)
    act = jnp.where(lane < GF0, i_log, jnp.where(lane < GDT0, f_log, jnp.where(lane < GEND, dt, 0.0)))
    a_row = -jnp.exp(alog_row) * LOG2E
    cs = jnp.where(lane < GF0, 0.0, jnp.where(lane < GDT0, act, jnp.where(lane < GEND, act * a_row, 0.0)))
    row_i = lax.broadcasted_iota(jnp.int32, (L, L), 0)
    col_i = lax.broadcasted_iota(jnp.int32, (L, L), 1)
    tril = (col_i <= row_i).astype(F32)
    cum = jnp.dot(tril, cs, precision=HIGHEST, preferred_element_type=F32)
    return act, cum


def _mlstm_state(k, v, bt_c, it_c, b_tot, m_prev, ct_ref, n_ref, m_ref, h):
    w_end = b_tot - bt_c + it_c
    m_loc = jnp.max(w_end, axis=0, keepdims=True)
    wgt = jnp.exp2(w_end - m_loc)
    vw = (v.astype(F32) * wgt).astype(BF16)
    s_loc = lax.dot_general(k, vw, (((0,), (0,)), ((), ())), preferred_element_type=F32)
    n_loc = jnp.sum(k.astype(F32) * wgt, axis=0, keepdims=True)
    m_new = jnp.maximum(b_tot + m_prev, m_loc)
    a_dec = jnp.exp2(b_tot + m_prev - m_new)
    s_dec = jnp.exp2(m_loc - m_new)
    ct_ref[h] = a_dec * ct_ref[h] + s_dec * s_loc
    n_ref[h:h + 1, :] = a_dec * n_ref[h:h + 1, :] + s_dec * n_loc
    m_ref[h:h + 1, :] = jnp.broadcast_to(m_new, (1, LANE))


def _head_expand(act, cum, expand_ref):
    L = act.shape[0]
    lane = lax.broadcasted_iota(jnp.int32, (L, GATE_W), 1)
    is_dt = (lane >= GDT0) & (lane < GEND)
    tot = cum[L - 1:L, :]
    e_cum = jnp.where(is_dt, jnp.exp2(cum), 0.0)
    e_end = jnp.where(is_dt, jnp.exp2(tot - cum) * act, 0.0)

    def split(x):
        hi = x.astype(BF16)
        lo = (x - hi.astype(F32)).astype(BF16)
        return jnp.concatenate([hi, lo], axis=1)

    lhs = jnp.concatenate([split(e_cum), split(e_end)], axis=0)
    return jnp.dot(lhs, expand_ref[...], preferred_element_type=F32)


def _chunk_step(r0, big_ref, gs, p, ct_ref, n_ref, m_ref, st_ref, hm_ref, ys_ref, hmask):
    L = CHUNK
    rows = pl.ds(r0, L)
    row_i = lax.broadcasted_iota(jnp.int32, (L, L), 0)
    col_i = lax.broadcasted_iota(jnp.int32, (L, L), 1)
    causal = col_i <= row_i
    neg_inf = -jnp.inf

    act = gs["act"][rows, :]
    cum = gs["cum"][rows, :]
    act_t = gs["act_t"][rows, :]
    cum_t = gs["cum_t"][rows, :]
    e_rows = pl.multiple_of(2 * r0, 2 * L)

    for h in range(M_HEADS):
        q = big_ref[rows, Q0 + h * M_DQK:Q0 + (h + 1) * M_DQK]
        k = big_ref[rows, K0 + h * M_DQK:K0 + (h + 1) * M_DQK]
        v = big_ref[rows, V0 + h * M_DV:V0 + (h + 1) * M_DV]
        gi, gf = GI0 + h, GF0 + h
        bt_c = cum[:, gf:gf + 1]
        bt_r = cum_t[gf:gf + 1, :]
        it_r = act_t[gi:gi + 1, :]
        it_c = act[:, gi:gi + 1]
        b_tot = cum[L - 1:L, gf:gf + 1]
        m_prev = m_ref[h:h + 1, 0:1]
        qk = lax.dot_general(q, k, (((1,), (1,)), ((), ())), preferred_element_type=F32)
        d_log = jnp.where(causal, bt_c - bt_r + it_r, neg_inf)
        mx = jnp.max(d_log, axis=1, keepdims=True)
        inter_log = bt_c + m_prev
        m_t = jnp.maximum(inter_log, mx)
        w_ts = jnp.exp2(d_log - m_t) * qk
        inter = jnp.exp2(inter_log - m_t)
        q_c = jnp.dot(q, ct_ref[h].astype(BF16), preferred_element_type=F32)
        num = jnp.dot(w_ts.astype(BF16), v, preferred_element_type=F32) + q_c * inter
        q_n = jnp.sum(q.astype(F32) * n_ref[h:h + 1, :], axis=1, keepdims=True)
        den = jnp.sum(w_ts, axis=1, keepdims=True) + inter * q_n
        denom = jnp.maximum(jnp.abs(den), jnp.exp2(-m_t))
        ms = jnp.mean(num * num, axis=1, keepdims=True)
        y = num * lax.rsqrt(ms + EPS * (denom * denom)) * p["mng"][:, h * M_DV:(h + 1) * M_DV]
        o_gate = big_ref[rows, O0 + h * M_DV:O0 + (h + 1) * M_DV]
        hm_ref[rows, h * M_DV:(h + 1) * M_DV] = y.astype(BF16) * o_gate
        _mlstm_state(k, v, bt_c, it_c, b_tot, m_prev, ct_ref, n_ref, m_ref, h)

    e_ref = gs["expanded"]
    for g in range(S_GROUPS):
        gcols = slice(g * S_GW, (g + 1) * S_GW)
        bg = big_ref[rows, X0 + XB0 + g * S_STATE:X0 + XB0 + (g + 1) * S_STATE]
        cg = big_ref[rows, X0 + XC0 + g * S_STATE:X0 + XC0 + (g + 1) * S_STATE]
        xg_b = big_ref[rows, X0 + g * S_GW:X0 + (g + 1) * S_GW]
        xg = xg_b.astype(F32)
        cbm = lax.dot_general(cg, bg, (((1,), (1,)), ((), ())), preferred_element_type=F32)
        y_diag = jnp.zeros((L, S_GW), F32)
        for j in range(S_HPG):
            gk = GDT0 + g * S_HPG + j
            ca_c = cum[:, gk:gk + 1]
            ca_r = cum_t[gk:gk + 1, :]
            dt_r = act_t[gk:gk + 1, :]
            dec = jnp.exp2(jnp.where(causal, ca_c - ca_r, neg_inf))
            w_m = (cbm * dec * dt_r).astype(BF16)
            y_diag = y_diag + jnp.dot(w_m, xg_b * hmask[j], preferred_element_type=F32)
        e_cum = e_ref[pl.ds(e_rows, L), gcols]
        e_end = e_ref[pl.ds(e_rows + L, L), gcols]
        e_tot = e_ref[pl.ds(e_rows + L - SUBLANE, SUBLANE), gcols][SUBLANE - 1:SUBLANE, :]
        y_off = jnp.dot(cg, st_ref[g].astype(BF16), preferred_element_type=F32) * e_cum
        y = y_diag + y_off + p["sd"][:, gcols] * xg
        yz = y * big_ref[rows, Z0 + g * S_GW:Z0 + (g + 1) * S_GW].astype(F32)
        ms = jnp.mean(yz * yz, axis=1, keepdims=True)
        ys_ref[rows, gcols] = (yz * lax.rsqrt(ms + EPS) * p["sng"][:, gcols]).astype(BF16)
        xw = (xg * e_end).astype(BF16)
        s_new = lax.dot_general(bg, xw, (((0,), (0,)), ((), ())), preferred_element_type=F32)
        st_ref[g] = st_ref[g] * e_tot + s_new


_SMALL_KEYS = ("gbias", "alog", "mng", "sd", "sng", "expand")


def _init_kernel(big_ref, gate_ref, gbias_ref, alog_ref, cw_ref, cb_ref, expand_ref,
                 ct_ref, n_ref, m_ref, st_ref, xh_ref):
    L = CHUNK
    act, cum = _gate_block(gate_ref[...] + gbias_ref[...], alog_ref[...], masked=True)
    ct_ref[...] = jnp.zeros_like(ct_ref)
    n_ref[...] = jnp.zeros_like(n_ref)
    m_ref[...] = jnp.zeros_like(m_ref)
    for h in range(M_HEADS):
        k = big_ref[:, K0 + h * M_DQK:K0 + (h + 1) * M_DQK]
        v = big_ref[:, V0 + h * M_DV:V0 + (h + 1) * M_DV]
        gi, gf = GI0 + h, GF0 + h
        _mlstm_state(k, v, cum[:, gf:gf + 1], act[:, gi:gi + 1], cum[L - 1:L, gf:gf + 1],
                     m_ref[h:h + 1, 0:1], ct_ref, n_ref, m_ref, h)

    e_all = _head_expand(act, cum, expand_ref)
    valid = lax.broadcasted_iota(jnp.int32, (L, S_XBC), 0) >= (L - N_META)
    xa = jnp.where(valid, big_ref[:, X0:X0 + S_XBC].astype(F32), 0.0)
    xh_ref[...] = xa[L - SUBLANE:L, :]
    xc = _conv_silu(xa, jnp.zeros((SUBLANE, S_XBC), F32), cw_ref, cb_ref, slice(0, S_XBC))
    for g in range(S_GROUPS):
        gcols = slice(g * S_GW, (g + 1) * S_GW)
        bg = xc[:, XB0 + g * S_STATE:XB0 + (g + 1) * S_STATE].astype(BF16)
        xw = (xc[:, gcols] * e_all[L:2 * L, gcols]).astype(BF16)
        st_ref[g] = lax.dot_general(bg, xw, (((0,), (0,)), ((), ())), preferred_element_type=F32)


_STATE_SHAPES = (
    (M_HEADS, M_DQK, M_DV),
    (SUBLANE, M_DQK),
    (SUBLANE, LANE),
    (S_GROUPS, S_STATE, S_GW),
    (SUBLANE, S_XBC),
)


def _init_state(big_m, gate_m, consts):
    return pl.pallas_call(
        _init_kernel,
        grid=(1,),
        in_specs=[_const_spec(big_m.shape), _const_spec(gate_m.shape)] + [_const_spec(a.shape) for a in consts],
        out_specs=[_const_spec(s) for s in _STATE_SHAPES],
        out_shape=[jax.ShapeDtypeStruct(s, F32) for s in _STATE_SHAPES],
        compiler_params=pltpu.CompilerParams(
            dimension_semantics=("arbitrary",), vmem_limit_bytes=VMEM_LIMIT),
        name="init_state",
    )(big_m, gate_m, *consts)


def _mixer_kernel(big_ref, gate_ref, x_ref, *refs):
    ns = len(_SMALL_KEYS)
    p = dict(zip(_SMALL_KEYS, refs[:ns]))
    ct0, n0, m0, st0 = refs[ns:ns + 4]
    mproj_ref, sproj_ref, wout_ref = refs[ns + 4:ns + 7]
    h1_ref = refs[ns + 7]
    ct_ref, n_ref, m_ref, st_ref, hm_ref, ys_ref, mg_ref = refs[ns + 8:ns + 15]
    gs = dict(zip(("act", "cum", "act_t", "cum_t", "expanded"), refs[ns + 15:]))

    @pl.when(pl.program_id(1) == 0)
    def _():
        ct_ref[...] = ct0[...]
        n_ref[...] = n0[...]
        m_ref[...] = m0[...]
        st_ref[...] = st0[...]

    for c in range(CPT):
        rs = slice(c * CHUNK, (c + 1) * CHUNK)
        act, cum = _gate_block(gate_ref[rs, :] + p["gbias"][...], p["alog"][...], masked=False)
        gs["act"][rs, :] = act
        gs["cum"][rs, :] = cum
        gs["act_t"][rs, :] = act.T
        gs["cum_t"][rs, :] = cum.T
        gs["expanded"][2 * c * CHUNK:2 * (c + 1) * CHUNK, :] = _head_expand(act, cum, p["expand"])

    lane_g = lax.broadcasted_iota(jnp.int32, (CHUNK, S_GW), 1)
    hmask = [((lane_g >= j * S_HEADDIM) & (lane_g < (j + 1) * S_HEADDIM)).astype(F32).astype(BF16)
             for j in range(S_HPG)]

    def body(c, carry):
        r0 = pl.multiple_of(c * CHUNK, CHUNK)
        _chunk_step(r0, big_ref, gs, p, ct_ref, n_ref, m_ref, st_ref, hm_ref, ys_ref, hmask)
        return carry

    for c in range(CPT):
        body(c, 0)

    for nb in range(D_MODEL // FB):
        cols = slice(nb * FB, (nb + 1) * FB)
        br_a = jnp.dot(hm_ref[...], mproj_ref[:, cols], preferred_element_type=F32)
        br_b = jnp.dot(ys_ref[...], sproj_ref[:, cols], preferred_element_type=F32)
        ga = big_ref[:, GA0 + nb * FB:GA0 + (nb + 1) * FB].astype(F32)
        gb = big_ref[:, GB0 + nb * FB:GB0 + (nb + 1) * FB].astype(F32)
        mg_ref[:, cols] = (ga * br_a + gb * br_b).astype(BF16)
    for nb in range(D_MODEL // FB):
        cols = slice(nb * FB, (nb + 1) * FB)
        h1_ref[:, cols] = x_ref[:, cols] + jnp.dot(mg_ref[...], wout_ref[:, cols], preferred_element_type=F32)


def _mixer(big, gate, x2, small, init, mproj, sproj, wout, batch):
    n = x2.shape[0]
    tiles = n // batch // TM
    row_map = lambda b, j: (b * tiles + j, 0)
    state_shapes = _STATE_SHAPES[:4]
    return pl.pallas_call(
        _mixer_kernel,
        grid=(batch, tiles),
        in_specs=[
            pl.BlockSpec((TM, BIG_W), row_map),
            pl.BlockSpec((TM, GATE_W), row_map),
            pl.BlockSpec((TM, D_MODEL), row_map),
        ] + [_const_spec(s.shape) for s in small]
          + [_const_spec(s) for s in state_shapes]
          + [_const_spec((D_MODEL, D_MODEL))] * 3,
        out_specs=pl.BlockSpec((TM, D_MODEL), row_map),
        out_shape=jax.ShapeDtypeStruct((n, D_MODEL), F32),
        scratch_shapes=[pltpu.VMEM(s, F32) for s in state_shapes] + [
            pltpu.VMEM((TM, M_V), BF16),
            pltpu.VMEM((TM, S_INNER), BF16),
            pltpu.VMEM((TM, D_MODEL), BF16),
            pltpu.VMEM((TM, GATE_W), F32),
            pltpu.VMEM((TM, GATE_W), F32),
            pltpu.VMEM((TM, GATE_W), F32),
            pltpu.VMEM((TM, GATE_W), F32),
            pltpu.VMEM((2 * TM, S_INNER), F32),
        ],
        compiler_params=pltpu.CompilerParams(
            dimension_semantics=("arbitrary", "arbitrary"), vmem_limit_bytes=VMEM_LIMIT),
        name="mixer",
    )(big, gate, x2, *small, *init[:4], mproj, sproj, wout)


def _ffn_kernel(h_ref, g2_ref, gf_ref, w1_ref, w2_ref, o_ref, u_ref, hid_ref):
    h = h_ref[...]
    ms = jnp.mean(h * h, axis=-1, keepdims=True)
    u_ref[...] = (h * lax.rsqrt(ms + EPS) * g2_ref[...]).astype(BF16)
    for jb in range(D_FF // FB):
        gate = jnp.dot(u_ref[...], w1_ref[:, jb * FB:(jb + 1) * FB], preferred_element_type=F32)
        up = jnp.dot(u_ref[...], w1_ref[:, D_FF + jb * FB:D_FF + (jb + 1) * FB], preferred_element_type=F32)
        hid_ref[:, jb * FB:(jb + 1) * FB] = (gate * jax.nn.sigmoid(gate) * up).astype(BF16)
    for nb in range(D_MODEL // FB):
        cols = slice(nb * FB, (nb + 1) * FB)
        o_ref[:, cols] = h_ref[:, cols] + jnp.dot(hid_ref[...], w2_ref[:, cols], preferred_element_type=F32)
    h2 = o_ref[...]
    ms2 = jnp.mean(h2 * h2, axis=-1, keepdims=True)
    o_ref[...] = h2 * lax.rsqrt(ms2 + EPS) * gf_ref[...]


def _ffn(h1, g2, gf, w1, w2):
    n = h1.shape[0]
    return pl.pallas_call(
        _ffn_kernel,
        grid=(n // TM,),
        in_specs=[
            pl.BlockSpec((TM, D_MODEL), lambda i: (i, 0)),
            _const_spec((1, D_MODEL)),
            _const_spec((1, D_MODEL)),
            _const_spec((D_MODEL, 2 * D_FF)),
            _const_spec((D_FF, D_MODEL)),
        ],
        out_specs=pl.BlockSpec((TM, D_MODEL), lambda i: (i, 0)),
        out_shape=jax.ShapeDtypeStruct((n, D_MODEL), F32),
        scratch_shapes=[pltpu.VMEM((TM, D_MODEL), BF16), pltpu.VMEM((TM, D_FF), BF16)],
        compiler_params=pltpu.CompilerParams(
            dimension_semantics=("arbitrary",), vmem_limit_bytes=VMEM_LIMIT),
        name="ffn",
    )(h1, g2, gf, w1, w2)


def _pad_lanes(row, width):
    return jnp.pad(row, ((0, 0), (0, width - row.shape[1])))


def kernel(x, meta, norm1_g, w_in, m_igate_b, m_fgate_b, m_norm_g, m_proj, s_conv_w, s_conv_b,
           s_dt_bias, s_A_log, s_D, s_norm_g, s_proj, w_out, norm2_g, w_ffn_in, w_ffn_out, norm_f_g):
    bsz, seq, dm = x.shape
    assert dm == D_MODEL and seq % TM == 0 and w_in.shape[0] == 1
    sizes = (M_QK, M_QK, M_V, M_V, M_HEADS, M_HEADS, S_INNER, S_XBC, S_HEADS, D_MODEL, D_MODEL)
    offs = [0]
    for s in sizes:
        offs.append(offs[-1] + s)
    wq, wk, wv, wo, wi, wf, wz, wx, wdt, wga, wgb = (w_in[0][:, offs[i]:offs[i + 1]] for i in range(len(sizes)))
    w_all = jnp.concatenate(
        [wq, wk, wv, wo, wz, wx, wga, wgb, wi, wf, wdt,
         jnp.zeros((D_MODEL, GATE_W - GEND), w_in.dtype)], axis=1).astype(BF16)

    gbias = _pad_lanes(jnp.concatenate([m_igate_b[0], m_fgate_b[0], s_dt_bias[0]])[None].astype(F32), GATE_W)
    alog = _pad_lanes(jnp.concatenate([jnp.zeros((GDT0,), F32), s_A_log[0].astype(F32)])[None], GATE_W)
    conv_w = jnp.pad(s_conv_w[0].astype(F32), ((0, SUBLANE - S_CONV), (0, 0)))
    conv_b = s_conv_b[0][None].astype(F32)
    head_of_lane = jnp.arange(S_INNER) // S_HEADDIM
    expand1 = (jnp.arange(GATE_W)[:, None] == GDT0 + head_of_lane[None, :]).astype(BF16)
    expand = jnp.concatenate([expand1, expand1], axis=0)
    small = (
        gbias, alog,
        m_norm_g[0].reshape(1, M_V).astype(F32),
        jnp.repeat(s_D[0].astype(F32), S_HEADDIM)[None],
        s_norm_g[0].reshape(1, S_INNER).astype(F32),
        expand,
    )
    g1 = norm1_g[0][None].astype(F32)

    meta_rows = jnp.concatenate([jnp.zeros((CHUNK - N_META, D_MODEL), x.dtype), meta.astype(x.dtype)], axis=0)
    big_m, gate_m = _inproj(meta_rows, g1, w_all, CHUNK)
    init = _init_state(big_m, gate_m, (gbias, alog, conv_w, conv_b, expand))

    x2 = x.reshape(bsz * seq, dm)
    big, gate = _inproj(x2, g1, w_all, TM, conv=(init[4], conv_w, conv_b), batch=bsz)
    h1 = _mixer(big, gate, x2, small, init, m_proj[0].astype(BF16), s_proj[0].astype(BF16),
                w_out[0].astype(BF16), bsz)
    out = _ffn(h1, norm2_g[0][None].astype(F32), norm_f_g[None].astype(F32),
               w_ffn_in[0].astype(BF16), w_ffn_out[0].astype(BF16))
    return out.reshape(bsz, seq, dm)
```

```python
import functools
import math

import jax
import jax.numpy as jnp
from jax import lax
from jax.experimental import pallas as pl
from jax.experimental.pallas import tpu as pltpu

F32 = jnp.float32
BF16 = jnp.bfloat16
HIGHEST = lax.Precision.HIGHEST
LOG2E = math.log2(math.e)

D_MODEL = 1024
N_META = 16
CHUNK = 128
EPS = 1e-6

M_HEADS = 4
M_DQK = 128
M_DV = 256
M_QK = M_HEADS * M_DQK
M_V = M_HEADS * M_DV
GATE_CAP = 15.0

S_HEADDIM = 64
S_INNER = D_MODEL
S_HEADS = S_INNER // S_HEADDIM
S_GROUPS = 4
S_HPG = S_HEADS // S_GROUPS
S_STATE = 128
S_CONV = 4
S_GW = S_HPG * S_HEADDIM
S_XBC = S_INNER + 2 * S_GROUPS * S_STATE
XB0 = S_INNER
XC0 = S_INNER + S_GROUPS * S_STATE

D_FF = 2816

Q0 = 0
K0 = Q0 + M_QK
V0 = K0 + M_QK
O0 = V0 + M_V
Z0 = O0 + M_V
X0 = Z0 + S_INNER
GA0 = X0 + S_XBC
GB0 = GA0 + D_MODEL
BIG_W = GB0 + D_MODEL
GATE_W = 128
GI0, GF0, GDT0, GEND = 0, M_HEADS, 2 * M_HEADS, 2 * M_HEADS + S_HEADS

LANE = 128
SUBLANE = 8
TM = 512
CPT = TM // CHUNK
NB = 512
FB = 256
VMEM_LIMIT = 56 * 1024 * 1024


def _softplus(x):
    return jnp.maximum(x, 0.0) + jnp.log1p(jnp.exp(-jnp.abs(x)))


def _const_spec(shape):
    nd = len(shape)
    return pl.BlockSpec(shape, lambda *_: (0,) * nd, pipeline_mode=pl.Buffered(1))


def _conv_silu(xa, hist, cw_ref, cb_ref, cols):
    row8 = lax.broadcasted_iota(jnp.int32, (SUBLANE, xa.shape[1]), 0)
    acc = cb_ref[:, cols] + xa * cw_ref[S_CONV - 1:S_CONV, cols]
    for sh in range(1, S_CONV):
        rolled = pltpu.roll(xa, sh, axis=0)
        top = jnp.where(row8 < sh, pltpu.roll(hist, sh, axis=0), rolled[0:SUBLANE])
        shifted = jnp.concatenate([top, rolled[SUBLANE:]], axis=0)
        acc = acc + shifted * cw_ref[S_CONV - 1 - sh:S_CONV - sh, cols]
    return acc * jax.nn.sigmoid(acc)


_IN_SIZES = (M_QK, M_QK, M_V, M_V, M_HEADS, M_HEADS, S_INNER, S_XBC, S_HEADS, D_MODEL, D_MODEL)
_IN_OFFS = tuple(sum(_IN_SIZES[:i]) for i in range(len(_IN_SIZES) + 1))
IN_WIDTH = _IN_OFFS[-1]
_PACK_SEGMENTS = (
    (_IN_OFFS[0], _IN_OFFS[4], Q0),
    (_IN_OFFS[6], _IN_OFFS[7], Z0),
    (_IN_OFFS[7], _IN_OFFS[8], X0),
    (_IN_OFFS[9], _IN_OFFS[10], GA0),
    (_IN_OFFS[10], _IN_OFFS[11], GB0),
)
PACK_ROWS = 512


def _pack_kernel(wt_ref, o_ref):
    for s0, s1, d0 in _PACK_SEGMENTS:
        for r in range(s0, s1, PACK_ROWS):
            o_ref[:, d0 + r - s0:d0 + r - s0 + PACK_ROWS] = wt_ref[r:r + PACK_ROWS, :].T.astype(BF16)
    kb = wt_ref.shape[1]
    gates = jnp.concatenate(
        [wt_ref[_IN_OFFS[4]:_IN_OFFS[6], :], wt_ref[_IN_OFFS[8]:_IN_OFFS[9], :],
         jnp.zeros((GATE_W - GEND, kb), F32)], axis=0)
    o_ref[:, BIG_W:BIG_W + GATE_W] = gates.T.astype(BF16)


def _pack_w_in(wt):
    kb = LANE
    return pl.pallas_call(
        _pack_kernel,
        grid=(D_MODEL // kb,),
        in_specs=[pl.BlockSpec((IN_WIDTH, kb), lambda i: (0, i))],
        out_specs=pl.BlockSpec((kb, BIG_W + GATE_W), lambda i: (i, 0)),
        out_shape=jax.ShapeDtypeStruct((D_MODEL, BIG_W + GATE_W), BF16),
        compiler_params=pltpu.CompilerParams(
            dimension_semantics=("arbitrary",), vmem_limit_bytes=VMEM_LIMIT),
        name="pack_w_in",
    )(wt)


def _inproj_kernel(x_ref, g_ref, w_ref, *refs, activate):
    if activate:
        xh0_ref, cw_ref, cb_ref, big_ref, gate_ref, u_ref, cs_ref = refs

        @pl.when(pl.program_id(1) == 0)
        def _():
            for s in range(S_XBC // LANE):
                cs_ref[s, 0:SUBLANE, :] = xh0_ref[:, s * LANE:(s + 1) * LANE]
    else:
        big_ref, gate_ref, u_ref = refs
    tm = x_ref.shape[0]
    xf = x_ref[...]
    ms = jnp.mean(xf * xf, axis=-1, keepdims=True)
    u_ref[...] = (xf * lax.rsqrt(ms + EPS) * g_ref[...]).astype(BF16)
    for n in range(BIG_W // NB):
        c0 = n * NB
        acc = jnp.dot(u_ref[...], w_ref[:, c0:c0 + NB], preferred_element_type=F32)
        if c0 < K0:
            acc = acc * (M_DQK ** -0.5)
        if activate:
            if O0 <= c0 < Z0 or c0 >= GA0:
                acc = jax.nn.sigmoid(acc)
            elif Z0 <= c0 < X0:
                acc = acc * jax.nn.sigmoid(acc)
            elif X0 <= c0 < GA0:
                for s in range(NB // LANE):
                    slab = (c0 - X0) // LANE + s
                    cols = slice((c0 - X0) + s * LANE, (c0 - X0) + (s + 1) * LANE)
                    cs_ref[slab, SUBLANE:SUBLANE + tm, :] = acc[:, s * LANE:(s + 1) * LANE]
                    conv = cb_ref[:, cols]
                    for tap in range(S_CONV):
                        back = S_CONV - 1 - tap
                        conv = conv + cs_ref[slab, pl.ds(SUBLANE - back, tm, stride=1), :] * cw_ref[tap:tap + 1, cols]
                    cs_ref[slab, 0:SUBLANE, :] = cs_ref[slab, tm:tm + SUBLANE, :]
                    big_ref[:, c0 + s * LANE:c0 + (s + 1) * LANE] = (conv * jax.nn.sigmoid(conv)).astype(BF16)
                continue
        big_ref[:, c0:c0 + NB] = acc.astype(BF16)
    gate_ref[...] = jnp.dot(u_ref[...], w_ref[:, BIG_W:BIG_W + GATE_W], preferred_element_type=F32)


def _inproj(rows, g, w_all, tm, conv=None, batch=1):
    n = rows.shape[0]
    tiles = n // batch // tm
    row_map = lambda b, j: (b * tiles + j, 0)
    activate = conv is not None
    extra = list(conv) if activate else []
    scratch = [pltpu.VMEM((tm, D_MODEL), BF16)]
    if activate:
        scratch.append(pltpu.VMEM((S_XBC // LANE, SUBLANE + tm, LANE), F32))
    return pl.pallas_call(
        functools.partial(_inproj_kernel, activate=activate),
        grid=(batch, tiles),
        in_specs=[
            pl.BlockSpec((tm, D_MODEL), row_map),
            _const_spec((1, D_MODEL)),
            _const_spec((D_MODEL, BIG_W + GATE_W)),
        ] + [_const_spec(a.shape) for a in extra],
        out_specs=[
            pl.BlockSpec((tm, BIG_W), row_map),
            pl.BlockSpec((tm, GATE_W), row_map),
        ],
        out_shape=[
            jax.ShapeDtypeStruct((n, BIG_W), BF16),
            jax.ShapeDtypeStruct((n, GATE_W), F32),
        ],
        scratch_shapes=scratch,
        compiler_params=pltpu.CompilerParams(
            dimension_semantics=("arbitrary", "arbitrary"), vmem_limit_bytes=VMEM_LIMIT),
        name="inproj_act" if activate else "inproj_raw",
    )(rows, g, w_all, *extra)


def _gate_block(pre, alog_row, masked):
    L = pre.shape[0]
    lane = lax.broadcasted_iota(jnp.int32, (L, GATE_W), 1)
    sc = GATE_CAP * jnp.tanh(pre / GATE_CAP)
    i_log = sc * LOG2E
    f_log = -_softplus(-sc) * LOG2E
    dt = _softplus(pre)
    if masked:
        valid = lax.broadcasted_iota(jnp.int32, (L, GATE_W), 0) >= (L - N_META)
        i_log = jnp.where(valid, i_log, -jnp.inf)
        f_log = jnp.where(valid, f_log, 0.0)
        dt = jnp.where(valid, dt, 0.0)
    act = jnp.where(lane < GF0, i_log, jnp.where(lane < GDT0, f_log, jnp.where(lane < GEND, dt, 0.0)))
    a_row = -jnp.exp(alog_row) * LOG2E
    cs = jnp.where(lane < GF0, 0.0, jnp.where(lane < GDT0, act, jnp.where(lane < GEND, act * a_row, 0.0)))
    row_i = lax.broadcasted_iota(jnp.int32, (L, L), 0)
    col_i = lax.broadcasted_iota(jnp.int32, (L, L), 1)
    tril = (col_i <= row_i).astype(F32)
    cum = jnp.dot(tril, cs, precision=HIGHEST, preferred_element_type=F32)
    return act, cum


def _mlstm_state(k, v, bt_c, it_c, b_tot, m_prev, ct_ref, n_ref, m_ref, h):
    w_end = b_tot - bt_c + it_c
    m_loc = jnp.max(w_end, axis=0, keepdims=True)
    wgt = jnp.exp2(w_end - m_loc)
    kw = k.astype(F32) * wgt
    s_loc = lax.dot_general(kw.astype(BF16), v, (((0,), (0,)), ((), ())), preferred_element_type=F32)
    n_loc = jnp.sum(kw, axis=0, keepdims=True)
    m_new = jnp.maximum(b_tot + m_prev, m_loc)
    a_dec = jnp.exp2(b_tot + m_prev - m_new)
    s_dec = jnp.exp2(m_loc - m_new)
    ct_ref[h] = a_dec * ct_ref[h] + s_dec * s_loc
    n_ref[h:h + 1, :] = a_dec * n_ref[h:h + 1, :] + s_dec * n_loc
    m_ref[h:h + 1, :] = jnp.broadcast_to(m_new, (1, LANE))


def _head_expand(act, cum, expand_ref):
    L = act.shape[0]
    lane = lax.broadcasted_iota(jnp.int32, (L, GATE_W), 1)
    is_dt = (lane >= GDT0) & (lane < GEND)
    tot = cum[L - 1:L, :]
    e_cum = jnp.where(is_dt, jnp.exp2(cum), 0.0)
    e_end = jnp.where(is_dt, jnp.exp2(tot - cum) * act, 0.0)

    def split(x):
        hi = x.astype(BF16)
        lo = (x - hi.astype(F32)).astype(BF16)
        return jnp.concatenate([hi, lo], axis=1)

    lhs = jnp.concatenate([split(e_cum), split(e_end)], axis=0)
    return jnp.dot(lhs, expand_ref[...], preferred_element_type=F32)


def _chunk_step(r0, big_ref, gs, p, ct_ref, n_ref, m_ref, st_ref, hm_ref, ys_ref, hmask):
    L = CHUNK
    rows = pl.ds(r0, L)
    row_i = lax.broadcasted_iota(jnp.int32, (L, L), 0)
    col_i = lax.broadcasted_iota(jnp.int32, (L, L), 1)
    causal = col_i <= row_i
    neg_inf = -jnp.inf

    act = gs["act"][rows, :]
    cum = gs["cum"][rows, :]
    act_t = gs["act_t"][rows, :]
    cum_t = gs["cum_t"][rows, :]
    e_rows = pl.multiple_of(2 * r0, 2 * L)

    zero_blk = jnp.zeros((L, LANE), BF16)

    def paired_nt(lhs_pair, rhs0, rhs1):
        rhs = jnp.concatenate([jnp.concatenate([rhs0, zero_blk], axis=1),
                               jnp.concatenate([zero_blk, rhs1], axis=1)], axis=0)
        out = lax.dot_general(lhs_pair, rhs, (((1,), (1,)), ((), ())), preferred_element_type=F32)
        return out[:, :LANE], out[:, LANE:]

    qk_heads = []
    for pr in range(M_HEADS // 2):
        qk_heads += paired_nt(
            big_ref[rows, Q0 + 2 * pr * M_DQK:Q0 + 2 * (pr + 1) * M_DQK],
            big_ref[rows, K0 + 2 * pr * M_DQK:K0 + (2 * pr + 1) * M_DQK],
            big_ref[rows, K0 + (2 * pr + 1) * M_DQK:K0 + 2 * (pr + 1) * M_DQK])

    for h in range(M_HEADS):
        q = big_ref[rows, Q0 + h * M_DQK:Q0 + (h + 1) * M_DQK]
        k = big_ref[rows, K0 + h * M_DQK:K0 + (h + 1) * M_DQK]
        v = big_ref[rows, V0 + h * M_DV:V0 + (h + 1) * M_DV]
        gi, gf = GI0 + h, GF0 + h
        bt_c = cum[:, gf:gf + 1]
        bt_r = cum_t[gf:gf + 1, :]
        it_r = act_t[gi:gi + 1, :]
        it_c = act[:, gi:gi + 1]
        b_tot = cum[L - 1:L, gf:gf + 1]
        m_prev = m_ref[h:h + 1, 0:1]
        qk = qk_heads[h]
        d_log = jnp.where(causal, bt_c - bt_r + it_r, neg_inf)
        mx = jnp.max(d_log, axis=1, keepdims=True)
        inter_log = bt_c + m_prev
        m_t = jnp.maximum(inter_log, mx)
        w_ts = jnp.exp2(d_log - m_t) * qk
        inter = jnp.exp2(inter_log - m_t)
        q_f = q.astype(F32)
        lhs = jnp.concatenate([w_ts.astype(BF16), (q_f * inter).astype(BF16)], axis=1)
        rhs = jnp.concatenate([v, ct_ref[h].astype(BF16)], axis=0)
        num = jnp.dot(lhs, rhs, preferred_element_type=F32)
        q_n = jnp.sum(q_f * n_ref[h:h + 1, :], axis=1, keepdims=True)
        den = jnp.sum(w_ts, axis=1, keepdims=True) + inter * q_n
        denom = jnp.maximum(jnp.abs(den), jnp.exp2(-m_t))
        ms = jnp.mean(num * num, axis=1, keepdims=True)
        y = num * lax.rsqrt(ms + EPS * (denom * denom)) * p["mng"][:, h * M_DV:(h + 1) * M_DV]
        o_gate = big_ref[rows, O0 + h * M_DV:O0 + (h + 1) * M_DV]
        hm_ref[rows, h * M_DV:(h + 1) * M_DV] = y.astype(BF16) * o_gate
        _mlstm_state(k, v, bt_c, it_c, b_tot, m_prev, ct_ref, n_ref, m_ref, h)

    e_ref = gs["expanded"]
    cb_groups = []
    for pr in range(S_GROUPS // 2):
        cb_groups += paired_nt(
            big_ref[rows, X0 + XC0 + 2 * pr * S_STATE:X0 + XC0 + 2 * (pr + 1) * S_STATE],
            big_ref[rows, X0 + XB0 + 2 * pr * S_STATE:X0 + XB0 + (2 * pr + 1) * S_STATE],
            big_ref[rows, X0 + XB0 + (2 * pr + 1) * S_STATE:X0 + XB0 + 2 * (pr + 1) * S_STATE])
    for g in range(S_GROUPS):
        gcols = slice(g * S_GW, (g + 1) * S_GW)
        bg = big_ref[rows, X0 + XB0 + g * S_STATE:X0 + XB0 + (g + 1) * S_STATE]
        cg = big_ref[rows, X0 + XC0 + g * S_STATE:X0 + XC0 + (g + 1) * S_STATE]
        xg_b = big_ref[rows, X0 + g * S_GW:X0 + (g + 1) * S_GW]
        xg = xg_b.astype(F32)
        cbm = cb_groups[g]
        w_heads = []
        for j in range(S_HPG):
            gk = GDT0 + g * S_HPG + j
            ca_c = cum[:, gk:gk + 1]
            ca_r = cum_t[gk:gk + 1, :]
            dt_r = act_t[gk:gk + 1, :]
            dec = jnp.exp2(jnp.where(causal, ca_c - ca_r, neg_inf))
            w_heads.append((cbm * dec * dt_r).astype(BF16))
        y_diag = jnp.zeros((L, S_GW), F32)
        for j in range(0, S_HPG, 2):
            y_diag = y_diag + jnp.dot(
                jnp.concatenate([w_heads[j], w_heads[j + 1]], axis=1),
                jnp.concatenate([xg_b * hmask[j], xg_b * hmask[j + 1]], axis=0),
                preferred_element_type=F32)
        e_cum = e_ref[pl.ds(e_rows, L), gcols]
        e_end = e_ref[pl.ds(e_rows + L, L), gcols]
        e_tot = e_ref[pl.ds(e_rows + L - SUBLANE, SUBLANE), gcols][SUBLANE - 1:SUBLANE, :]
        y_off = jnp.dot(cg, st_ref[g].astype(BF16), preferred_element_type=F32) * e_cum
        y = y_diag + y_off + p["sd"][:, gcols] * xg
        yz = y * big_ref[rows, Z0 + g * S_GW:Z0 + (g + 1) * S_GW].astype(F32)
        ms = jnp.mean(yz * yz, axis=1, keepdims=True)
        ys_ref[rows, gcols] = (yz * lax.rsqrt(ms + EPS) * p["sng"][:, gcols]).astype(BF16)
        xw = (xg * e_end).astype(BF16)
        s_new = lax.dot_general(bg, xw, (((0,), (0,)), ((), ())), preferred_element_type=F32)
        st_ref[g] = st_ref[g] * e_tot + s_new


_SMALL_KEYS = ("gbias", "alog", "mng", "sd", "sng", "expand")


def _init_kernel(big_ref, gate_ref, gbias_ref, alog_ref, cw_ref, cb_ref, expand_ref,
                 ct_ref, n_ref, m_ref, st_ref, xh_ref):
    L = CHUNK
    act, cum = _gate_block(gate_ref[...] + gbias_ref[...], alog_ref[...], masked=True)
    ct_ref[...] = jnp.zeros_like(ct_ref)
    n_ref[...] = jnp.zeros_like(n_ref)
    m_ref[...] = jnp.zeros_like(m_ref)
    for h in range(M_HEADS):
        k = big_ref[:, K0 + h * M_DQK:K0 + (h + 1) * M_DQK]
        v = big_ref[:, V0 + h * M_DV:V0 + (h + 1) * M_DV]
        gi, gf = GI0 + h, GF0 + h
        _mlstm_state(k, v, cum[:, gf:gf + 1], act[:, gi:gi + 1], cum[L - 1:L, gf:gf + 1],
                     m_ref[h:h + 1, 0:1], ct_ref, n_ref, m_ref, h)

    e_all = _head_expand(act, cum, expand_ref)
    valid = lax.broadcasted_iota(jnp.int32, (L, S_XBC), 0) >= (L - N_META)
    xa = jnp.where(valid, big_ref[:, X0:X0 + S_XBC].astype(F32), 0.0)
    xh_ref[...] = xa[L - SUBLANE:L, :]
    xc = _conv_silu(xa, jnp.zeros((SUBLANE, S_XBC), F32), cw_ref, cb_ref, slice(0, S_XBC))
    for g in range(S_GROUPS):
        gcols = slice(g * S_GW, (g + 1) * S_GW)
        bg = xc[:, XB0 + g * S_STATE:XB0 + (g + 1) * S_STATE].astype(BF16)
        xw = (xc[:, gcols] * e_all[L:2 * L, gcols]).astype(BF16)
        st_ref[g] = lax.dot_general(bg, xw, (((0,), (0,)), ((), ())), preferred_element_type=F32)


_STATE_SHAPES = (
    (M_HEADS, M_DQK, M_DV),
    (SUBLANE, M_DQK),
    (SUBLANE, LANE),
    (S_GROUPS, S_STATE, S_GW),
    (SUBLANE, S_XBC),
)


def _init_state(big_m, gate_m, consts):
    return pl.pallas_call(
        _init_kernel,
        grid=(1,),
        in_specs=[_const_spec(big_m.shape), _const_spec(gate_m.shape)] + [_const_spec(a.shape) for a in consts],
        out_specs=[_const_spec(s) for s in _STATE_SHAPES],
        out_shape=[jax.ShapeDtypeStruct(s, F32) for s in _STATE_SHAPES],
        compiler_params=pltpu.CompilerParams(
            dimension_semantics=("arbitrary",), vmem_limit_bytes=VMEM_LIMIT),
        name="init_state",
    )(big_m, gate_m, *consts)


def _mixer_kernel(big_ref, gate_ref, x_ref, *refs):
    ns = len(_SMALL_KEYS)
    p = dict(zip(_SMALL_KEYS, refs[:ns]))
    ct0, n0, m0, st0 = refs[ns:ns + 4]
    mproj_ref, sproj_ref, wout_ref = refs[ns + 4:ns + 7]
    h1_ref = refs[ns + 7]
    ct_ref, n_ref, m_ref, st_ref, hm_ref, ys_ref, mg_ref = refs[ns + 8:ns + 15]
    gs = dict(zip(("act", "cum", "act_t", "cum_t", "expanded"), refs[ns + 15:]))

    @pl.when(pl.program_id(1) == 0)
    def _():
        ct_ref[...] = ct0[...]
        n_ref[...] = n0[...]
        m_ref[...] = m0[...]
        st_ref[...] = st0[...]

    for c in range(CPT):
        rs = slice(c * CHUNK, (c + 1) * CHUNK)
        act, cum = _gate_block(gate_ref[rs, :] + p["gbias"][...], p["alog"][...], masked=False)
        gs["act"][rs, :] = act
        gs["cum"][rs, :] = cum
        gs["act_t"][rs, :] = act.T
        gs["cum_t"][rs, :] = cum.T
        gs["expanded"][2 * c * CHUNK:2 * (c + 1) * CHUNK, :] = _head_expand(act, cum, p["expand"])

    lane_g = lax.broadcasted_iota(jnp.int32, (CHUNK, S_GW), 1)
    hmask = [((lane_g >= j * S_HEADDIM) & (lane_g < (j + 1) * S_HEADDIM)).astype(F32).astype(BF16)
             for j in range(S_HPG)]

    def body(c, carry):
        r0 = pl.multiple_of(c * CHUNK, CHUNK)
        _chunk_step(r0, big_ref, gs, p, ct_ref, n_ref, m_ref, st_ref, hm_ref, ys_ref, hmask)
        return carry

    for c in range(CPT):
        body(c, 0)

    for nb in range(D_MODEL // FB):
        cols = slice(nb * FB, (nb + 1) * FB)
        br_a = jnp.dot(hm_ref[...], mproj_ref[:, cols], preferred_element_type=F32)
        br_b = jnp.dot(ys_ref[...], sproj_ref[:, cols], preferred_element_type=F32)
        ga = big_ref[:, GA0 + nb * FB:GA0 + (nb + 1) * FB].astype(F32)
        gb = big_ref[:, GB0 + nb * FB:GB0 + (nb + 1) * FB].astype(F32)
        mg_ref[:, cols] = (ga * br_a + gb * br_b).astype(BF16)
    for nb in range(D_MODEL // FB):
        cols = slice(nb * FB, (nb + 1) * FB)
        h1_ref[:, cols] = x_ref[:, cols] + jnp.dot(mg_ref[...], wout_ref[:, cols], preferred_element_type=F32)


def _mixer(big, gate, x2, small, init, mproj, sproj, wout, batch):
    n = x2.shape[0]
    tiles = n // batch // TM
    row_map = lambda b, j: (b * tiles + j, 0)
    state_shapes = _STATE_SHAPES[:4]
    return pl.pallas_call(
        _mixer_kernel,
        grid=(batch, tiles),
        in_specs=[
            pl.BlockSpec((TM, BIG_W), row_map),
            pl.BlockSpec((TM, GATE_W), row_map),
            pl.BlockSpec((TM, D_MODEL), row_map),
        ] + [_const_spec(s.shape) for s in small]
          + [_const_spec(s) for s in state_shapes]
          + [_const_spec((D_MODEL, D_MODEL))] * 3,
        out_specs=pl.BlockSpec((TM, D_MODEL), row_map),
        out_shape=jax.ShapeDtypeStruct((n, D_MODEL), F32),
        scratch_shapes=[pltpu.VMEM(s, F32) for s in state_shapes] + [
            pltpu.VMEM((TM, M_V), BF16),
            pltpu.VMEM((TM, S_INNER), BF16),
            pltpu.VMEM((TM, D_MODEL), BF16),
            pltpu.VMEM((TM, GATE_W), F32),
            pltpu.VMEM((TM, GATE_W), F32),
            pltpu.VMEM((TM, GATE_W), F32),
            pltpu.VMEM((TM, GATE_W), F32),
            pltpu.VMEM((2 * TM, S_INNER), F32),
        ],
        compiler_params=pltpu.CompilerParams(
            dimension_semantics=("arbitrary", "arbitrary"), vmem_limit_bytes=VMEM_LIMIT),
        name="mixer",
    )(big, gate, x2, *small, *init[:4], mproj, sproj, wout)


def _ffn_kernel(h_ref, g2_ref, gf_ref, w1_ref, w2_ref, o_ref, u_ref, hid_ref):
    h = h_ref[...]
    ms = jnp.mean(h * h, axis=-1, keepdims=True)
    u_ref[...] = (h * lax.rsqrt(ms + EPS) * g2_ref[...]).astype(BF16)
    for jb in range(D_FF // FB):
        gate = jnp.dot(u_ref[...], w1_ref[:, jb * FB:(jb + 1) * FB], preferred_element_type=F32)
        up = jnp.dot(u_ref[...], w1_ref[:, D_FF + jb * FB:D_FF + (jb + 1) * FB], preferred_element_type=F32)
        hid_ref[:, jb * FB:(jb + 1) * FB] = (gate * jax.nn.sigmoid(gate) * up).astype(BF16)
    for nb in range(D_MODEL // FB):
        cols = slice(nb * FB, (nb + 1) * FB)
        o_ref[:, cols] = h_ref[:, cols] + jnp.dot(hid_ref[...], w2_ref[:, cols], preferred_element_type=F32)
    h2 = o_ref[...]
    ms2 = jnp.mean(h2 * h2, axis=-1, keepdims=True)
    o_ref[...] = h2 * lax.rsqrt(ms2 + EPS) * gf_ref[...]


def _ffn(h1, g2, gf, w1, w2):
    n = h1.shape[0]
    return pl.pallas_call(
        _ffn_kernel,
        grid=(n // TM,),
        in_specs=[
            pl.BlockSpec((TM, D_MODEL), lambda i: (i, 0)),
            _const_spec((1, D_MODEL)),
            _const_spec((1, D_MODEL)),
            _const_spec((D_MODEL, 2 * D_FF)),
            _const_spec((D_FF, D_MODEL)),
        ],
        out_specs=pl.BlockSpec((TM, D_MODEL), lambda i: (i, 0)),
        out_shape=jax.ShapeDtypeStruct((n, D_MODEL), F32),
        scratch_shapes=[pltpu.VMEM((TM, D_MODEL), BF16), pltpu.VMEM((TM, D_FF), BF16)],
        compiler_params=pltpu.CompilerParams(
            dimension_semantics=("arbitrary",), vmem_limit_bytes=VMEM_LIMIT),
        name="ffn",
    )(h1, g2, gf, w1, w2)


def _pad_lanes(row, width):
    return jnp.pad(row, ((0, 0), (0, width - row.shape[1])))


def kernel(x, meta, norm1_g, w_in, m_igate_b, m_fgate_b, m_norm_g, m_proj, s_conv_w, s_conv_b,
           s_dt_bias, s_A_log, s_D, s_norm_g, s_proj, w_out, norm2_g, w_ffn_in, w_ffn_out, norm_f_g):
    bsz, seq, dm = x.shape
    assert dm == D_MODEL and seq % TM == 0 and w_in.shape[0] == 1
    w_all = _pack_w_in(w_in[0].T)

    gbias = _pad_lanes(jnp.concatenate([m_igate_b[0], m_fgate_b[0], s_dt_bias[0]])[None].astype(F32), GATE_W)
    alog = _pad_lanes(jnp.concatenate([jnp.zeros((GDT0,), F32), s_A_log[0].astype(F32)])[None], GATE_W)
    conv_w = jnp.pad(s_conv_w[0].astype(F32), ((0, SUBLANE - S_CONV), (0, 0)))
    conv_b = s_conv_b[0][None].astype(F32)
    head_of_lane = jnp.arange(S_INNER) // S_HEADDIM
    expand1 = (jnp.arange(GATE_W)[:, None] == GDT0 + head_of_lane[None, :]).astype(BF16)
    expand = jnp.concatenate([expand1, expand1], axis=0)
    small = (
        gbias, alog,
        m_norm_g[0].reshape(1, M_V).astype(F32),
        jnp.repeat(s_D[0].astype(F32), S_HEADDIM)[None],
        s_norm_g[0].reshape(1, S_INNER).astype(F32),
        expand,
    )
    g1 = norm1_g[0][None].astype(F32)

    meta_rows = jnp.concatenate([jnp.zeros((CHUNK - N_META, D_MODEL), x.dtype), meta.astype(x.dtype)], axis=0)
    big_m, gate_m = _inproj(meta_rows, g1, w_all, CHUNK)
    init = _init_state(big_m, gate_m, (gbias, alog, conv_w, conv_b, expand))

    x2 = x.reshape(bsz * seq, dm)
    big, gate = _inproj(x2, g1, w_all, TM, conv=(init[4], conv_w, conv_b), batch=bsz)
    h1 = _mixer(big, gate, x2, small, init, m_proj[0].astype(BF16), s_proj[0].astype(BF16),
                w_out[0].astype(BF16), bsz)
    out = _ffn(h1, norm2_g[0][None].astype(F32), norm_f_g[None].astype(F32),
               w_ffn_in[0].astype(BF16), w_ffn_out[0].astype(BF16))
    return out.reshape(bsz, seq, dm)
```

```python
import functools
import math

import jax
import jax.numpy as jnp
from jax import lax
from jax.experimental import pallas as pl
from jax.experimental.pallas import tpu as pltpu

F32 = jnp.float32
BF16 = jnp.bfloat16
HIGHEST = lax.Precision.HIGHEST
LOG2E = math.log2(math.e)

D_MODEL = 1024
N_META = 16
CHUNK = 128
EPS = 1e-6

M_HEADS = 4
M_DQK = 128
M_DV = 256
M_QK = M_HEADS * M_DQK
M_V = M_HEADS * M_DV
GATE_CAP = 15.0

S_HEADDIM = 64
S_INNER = D_MODEL
S_HEADS = S_INNER // S_HEADDIM
S_GROUPS = 4
S_HPG = S_HEADS // S_GROUPS
S_STATE = 128
S_CONV = 4
S_GW = S_HPG * S_HEADDIM
S_XBC = S_INNER + 2 * S_GROUPS * S_STATE
XB0 = S_INNER
XC0 = S_INNER + S_GROUPS * S_STATE

D_FF = 2816

Q0 = 0
K0 = Q0 + M_QK
V0 = K0 + M_QK
O0 = V0 + M_V
Z0 = O0 + M_V
X0 = Z0 + S_INNER
GA0 = X0 + S_XBC
GB0 = GA0 + D_MODEL
BIG_W = GB0 + D_MODEL
GATE_W = 128
GI0, GF0, GDT0, GEND = 0, M_HEADS, 2 * M_HEADS, 2 * M_HEADS + S_HEADS

LANE = 128
SUBLANE = 8
TM = 512
MIX_TM = 512
CPT = MIX_TM // CHUNK
NB = 512
FB = 256
VMEM_LIMIT = 56 * 1024 * 1024


def _softplus(x):
    return jnp.maximum(x, 0.0) + jnp.log1p(jnp.exp(-jnp.abs(x)))


def _const_spec(shape):
    nd = len(shape)
    return pl.BlockSpec(shape, lambda *_: (0,) * nd, pipeline_mode=pl.Buffered(1))


def _conv_silu(xa, hist, cw_ref, cb_ref, cols):
    row8 = lax.broadcasted_iota(jnp.int32, (SUBLANE, xa.shape[1]), 0)
    acc = cb_ref[:, cols] + xa * cw_ref[S_CONV - 1:S_CONV, cols]
    for sh in range(1, S_CONV):
        rolled = pltpu.roll(xa, sh, axis=0)
        top = jnp.where(row8 < sh, pltpu.roll(hist, sh, axis=0), rolled[0:SUBLANE])
        shifted = jnp.concatenate([top, rolled[SUBLANE:]], axis=0)
        acc = acc + shifted * cw_ref[S_CONV - 1 - sh:S_CONV - sh, cols]
    return acc * jax.nn.sigmoid(acc)


_IN_SIZES = (M_QK, M_QK, M_V, M_V, M_HEADS, M_HEADS, S_INNER, S_XBC, S_HEADS, D_MODEL, D_MODEL)
_IN_OFFS = tuple(sum(_IN_SIZES[:i]) for i in range(len(_IN_SIZES) + 1))
IN_WIDTH = _IN_OFFS[-1]
_PACK_SEGMENTS = (
    (_IN_OFFS[0], _IN_OFFS[4], Q0),
    (_IN_OFFS[6], _IN_OFFS[7], Z0),
    (_IN_OFFS[7], _IN_OFFS[8], X0),
    (_IN_OFFS[9], _IN_OFFS[10], GA0),
    (_IN_OFFS[10], _IN_OFFS[11], GB0),
)
PACK_ROWS = 512


def _pack_kernel(wt_ref, o_ref):
    for s0, s1, d0 in _PACK_SEGMENTS:
        for r in range(s0, s1, PACK_ROWS):
            o_ref[:, d0 + r - s0:d0 + r - s0 + PACK_ROWS] = wt_ref[r:r + PACK_ROWS, :].T.astype(BF16)
    kb = wt_ref.shape[1]
    gates = jnp.concatenate(
        [wt_ref[_IN_OFFS[4]:_IN_OFFS[6], :], wt_ref[_IN_OFFS[8]:_IN_OFFS[9], :],
         jnp.zeros((GATE_W - GEND, kb), F32)], axis=0)
    o_ref[:, BIG_W:BIG_W + GATE_W] = gates.T.astype(BF16)


def _pack_w_in(wt):
    kb = LANE
    return pl.pallas_call(
        _pack_kernel,
        grid=(D_MODEL // kb,),
        in_specs=[pl.BlockSpec((IN_WIDTH, kb), lambda i: (0, i))],
        out_specs=pl.BlockSpec((kb, BIG_W + GATE_W), lambda i: (i, 0)),
        out_shape=jax.ShapeDtypeStruct((D_MODEL, BIG_W + GATE_W), BF16),
        compiler_params=pltpu.CompilerParams(
            dimension_semantics=("arbitrary",), vmem_limit_bytes=VMEM_LIMIT),
        name="pack_w_in",
    )(wt)


def _inproj_kernel(x_ref, g_ref, w_ref, *refs, activate):
    if activate:
        xh0_ref, cw_ref, cb_ref, big_ref, gate_ref, u_ref, cs_ref = refs

        @pl.when(pl.program_id(1) == 0)
        def _():
            for s in range(S_XBC // LANE):
                cs_ref[s, 0:SUBLANE, :] = xh0_ref[:, s * LANE:(s + 1) * LANE]
    else:
        big_ref, gate_ref, u_ref = refs
    tm = x_ref.shape[0]
    xf = x_ref[...]
    ms = jnp.mean(xf * xf, axis=-1, keepdims=True)
    u_ref[...] = (xf * lax.rsqrt(ms + EPS) * g_ref[...]).astype(BF16)
    for n in range(BIG_W // NB):
        c0 = n * NB
        acc = jnp.dot(u_ref[...], w_ref[:, c0:c0 + NB], preferred_element_type=F32)
        if c0 < K0:
            acc = acc * (M_DQK ** -0.5)
        if activate:
            if O0 <= c0 < Z0 or c0 >= GA0:
                acc = jax.nn.sigmoid(acc)
            elif Z0 <= c0 < X0:
                acc = acc * jax.nn.sigmoid(acc)
            elif X0 <= c0 < GA0:
                for s in range(NB // LANE):
                    slab = (c0 - X0) // LANE + s
                    cols = slice((c0 - X0) + s * LANE, (c0 - X0) + (s + 1) * LANE)
                    cs_ref[slab, SUBLANE:SUBLANE + tm, :] = acc[:, s * LANE:(s + 1) * LANE]
                    conv = cb_ref[:, cols]
                    for tap in range(S_CONV):
                        back = S_CONV - 1 - tap
                        conv = conv + cs_ref[slab, pl.ds(SUBLANE - back, tm, stride=1), :] * cw_ref[tap:tap + 1, cols]
                    cs_ref[slab, 0:SUBLANE, :] = cs_ref[slab, tm:tm + SUBLANE, :]
                    big_ref[:, c0 + s * LANE:c0 + (s + 1) * LANE] = (conv * jax.nn.sigmoid(conv)).astype(BF16)
                continue
        big_ref[:, c0:c0 + NB] = acc.astype(BF16)
    gate_ref[...] = jnp.dot(u_ref[...], w_ref[:, BIG_W:BIG_W + GATE_W], preferred_element_type=F32)


def _inproj(rows, g, w_all, tm, conv=None, batch=1):
    n = rows.shape[0]
    tiles = n // batch // tm
    row_map = lambda b, j: (b * tiles + j, 0)
    activate = conv is not None
    extra = list(conv) if activate else []
    scratch = [pltpu.VMEM((tm, D_MODEL), BF16)]
    if activate:
        scratch.append(pltpu.VMEM((S_XBC // LANE, SUBLANE + tm, LANE), F32))
    return pl.pallas_call(
        functools.partial(_inproj_kernel, activate=activate),
        grid=(batch, tiles),
        in_specs=[
            pl.BlockSpec((tm, D_MODEL), row_map),
            _const_spec((1, D_MODEL)),
            _const_spec((D_MODEL, BIG_W + GATE_W)),
        ] + [_const_spec(a.shape) for a in extra],
        out_specs=[
            pl.BlockSpec((tm, BIG_W), row_map),
            pl.BlockSpec((tm, GATE_W), row_map),
        ],
        out_shape=[
            jax.ShapeDtypeStruct((n, BIG_W), BF16),
            jax.ShapeDtypeStruct((n, GATE_W), F32),
        ],
        scratch_shapes=scratch,
        compiler_params=pltpu.CompilerParams(
            dimension_semantics=("arbitrary", "arbitrary"), vmem_limit_bytes=VMEM_LIMIT),
        name="inproj_act" if activate else "inproj_raw",
    )(rows, g, w_all, *extra)


def _gate_block(pre, alog_row, masked):
    L = pre.shape[0]
    lane = lax.broadcasted_iota(jnp.int32, (L, GATE_W), 1)
    sc = GATE_CAP * jnp.tanh(pre / GATE_CAP)
    i_log = sc * LOG2E
    f_log = -_softplus(-sc) * LOG2E
    dt = _softplus(pre)
    if masked:
        valid = lax.broadcasted_iota(jnp.int32, (L, GATE_W), 0) >= (L - N_META)
        i_log = jnp.where(valid, i_log, -jnp.inf)
        f_log = jnp.where(valid, f_log, 0.0)
        dt = jnp.where(valid, dt, 0.0)
    act = jnp.where(lane < GF0, i_log, jnp.where(lane < GDT0, f_log, jnp.where(lane < GEND, dt, 0.0)))
    a_row = -jnp.exp(alog_row) * LOG2E
    cs = jnp.where(lane < GF0, 0.0, jnp.where(lane < GDT0, act, jnp.where(lane < GEND, act * a_row, 0.0)))
    row_i = lax.broadcasted_iota(jnp.int32, (L, L), 0)
    col_i = lax.broadcasted_iota(jnp.int32, (L, L), 1)
    tril = (col_i <= row_i).astype(F32).astype(BF16)
    hi = cs.astype(BF16)
    rem = cs - hi.astype(F32)
    mid = rem.astype(BF16)
    lo = (rem - mid.astype(F32)).astype(BF16)
    cum = (jnp.dot(jnp.concatenate([tril, tril], axis=1), jnp.concatenate([hi, mid], axis=0),
                   preferred_element_type=F32)
           + jnp.dot(tril, lo, preferred_element_type=F32))
    return act, cum


def _lane_rep(x, col):
    return jnp.broadcast_to(x[:, col:col + 1], (x.shape[0], LANE))


def _mlstm_state(k, v, bt_b, it_b, b_tot, m_prev, ct_ref, n_ref, m_ref, h):
    w_end = b_tot - bt_b + it_b
    m_loc = jnp.max(w_end, axis=0, keepdims=True)
    wgt = jnp.exp2(w_end - m_loc)
    kw = k.astype(F32) * wgt
    s_loc = lax.dot_general(kw.astype(BF16), v, (((0,), (0,)), ((), ())), preferred_element_type=F32)
    n_loc = jnp.sum(kw, axis=0, keepdims=True)
    m_new = jnp.maximum(b_tot + m_prev, m_loc)
    a_dec = jnp.exp2(b_tot + m_prev - m_new)
    s_dec = jnp.exp2(m_loc - m_new)
    ct_ref[h] = (jnp.concatenate([a_dec, a_dec], axis=1) * ct_ref[h]
                 + jnp.concatenate([s_dec, s_dec], axis=1) * s_loc)
    n_ref[h:h + 1, :] = a_dec * n_ref[h:h + 1, :] + s_dec * n_loc
    m_ref[h:h + 1, :] = m_new


def _head_expand(act, cum, expand_ref):
    L = act.shape[0]
    lane = lax.broadcasted_iota(jnp.int32, (L, GATE_W), 1)
    is_dt = (lane >= GDT0) & (lane < GEND)
    tot = cum[L - 1:L, :]
    e_cum = jnp.where(is_dt, jnp.exp2(cum), 0.0)
    e_end = jnp.where(is_dt, jnp.exp2(tot - cum) * act, 0.0)

    def split(x):
        hi = x.astype(BF16)
        lo = (x - hi.astype(F32)).astype(BF16)
        return jnp.concatenate([hi, lo], axis=1)

    lhs = jnp.concatenate([split(e_cum), split(e_end)], axis=0)
    return jnp.dot(lhs, expand_ref[...], preferred_element_type=F32)


def _chunk_step(r0, big_ref, gs, p, ct_ref, n_ref, m_ref, st_ref, hm_ref, ys_ref, hmask):
    L = CHUNK
    rows = pl.ds(r0, L)
    row_i = lax.broadcasted_iota(jnp.int32, (L, L), 0)
    col_i = lax.broadcasted_iota(jnp.int32, (L, L), 1)
    causal = col_i <= row_i
    neg_inf = -jnp.inf

    rep_ref = gs["rep"]
    act_t = gs["act_t"][rows, :]
    cum_t = gs["cum_t"][rows, :]
    e_rows = pl.multiple_of(2 * r0, 2 * L)

    zero_blk = jnp.zeros((L, LANE), BF16)

    def paired_nt(lhs_pair, rhs0, rhs1):
        rhs = jnp.concatenate([jnp.concatenate([rhs0, zero_blk], axis=1),
                               jnp.concatenate([zero_blk, rhs1], axis=1)], axis=0)
        out = lax.dot_general(lhs_pair, rhs, (((1,), (1,)), ((), ())), preferred_element_type=F32)
        return out[:, :LANE], out[:, LANE:]

    qk_heads = []
    for pr in range(M_HEADS // 2):
        qk_heads += paired_nt(
            big_ref[rows, Q0 + 2 * pr * M_DQK:Q0 + 2 * (pr + 1) * M_DQK],
            big_ref[rows, K0 + 2 * pr * M_DQK:K0 + (2 * pr + 1) * M_DQK],
            big_ref[rows, K0 + (2 * pr + 1) * M_DQK:K0 + 2 * (pr + 1) * M_DQK])

    for h in range(M_HEADS):
        q = big_ref[rows, Q0 + h * M_DQK:Q0 + (h + 1) * M_DQK]
        k = big_ref[rows, K0 + h * M_DQK:K0 + (h + 1) * M_DQK]
        v = big_ref[rows, V0 + h * M_DV:V0 + (h + 1) * M_DV]
        gi, gf = GI0 + h, GF0 + h
        bt_b = rep_ref[h, rows, :]
        it_b = rep_ref[M_HEADS + h, rows, :]
        bt_r = cum_t[gf:gf + 1, :]
        it_r = act_t[gi:gi + 1, :]
        b_tot = bt_b[L - 1:L, :]
        m_prev = m_ref[h:h + 1, :]
        qk = qk_heads[h]
        d_log = jnp.where(causal, bt_b - bt_r + it_r, neg_inf)
        mx = jnp.max(d_log, axis=1, keepdims=True)
        inter_log = bt_b + m_prev
        m_t = jnp.maximum(inter_log, mx)
        w_ts = jnp.exp2(d_log - m_t) * qk
        inter = jnp.exp2(inter_log - m_t)
        q_f = q.astype(F32)
        lhs = jnp.concatenate([w_ts.astype(BF16), (q_f * inter).astype(BF16)], axis=1)
        rhs = jnp.concatenate([v, ct_ref[h].astype(BF16)], axis=0)
        num = jnp.dot(lhs, rhs, preferred_element_type=F32)
        q_n = jnp.sum(q_f * n_ref[h:h + 1, :], axis=1, keepdims=True)
        den = jnp.sum(w_ts, axis=1, keepdims=True) + inter * q_n
        denom = jnp.maximum(jnp.abs(den), jnp.exp2(-m_t))
        ms = jnp.mean(num * num, axis=1, keepdims=True)
        rs = lax.rsqrt(ms + EPS * (denom * denom))
        y = num * jnp.concatenate([rs, rs], axis=1) * p["mng"][:, h * M_DV:(h + 1) * M_DV]
        o_gate = big_ref[rows, O0 + h * M_DV:O0 + (h + 1) * M_DV]
        hm_ref[rows, h * M_DV:(h + 1) * M_DV] = y.astype(BF16) * o_gate
        _mlstm_state(k, v, bt_b, it_b, b_tot, m_prev, ct_ref, n_ref, m_ref, h)

    e_ref = gs["expanded"]
    cb_groups = []
    for pr in range(S_GROUPS // 2):
        cb_groups += paired_nt(
            big_ref[rows, X0 + XC0 + 2 * pr * S_STATE:X0 + XC0 + 2 * (pr + 1) * S_STATE],
            big_ref[rows, X0 + XB0 + 2 * pr * S_STATE:X0 + XB0 + (2 * pr + 1) * S_STATE],
            big_ref[rows, X0 + XB0 + (2 * pr + 1) * S_STATE:X0 + XB0 + 2 * (pr + 1) * S_STATE])
    for g in range(S_GROUPS):
        gcols = slice(g * S_GW, (g + 1) * S_GW)
        bg = big_ref[rows, X0 + XB0 + g * S_STATE:X0 + XB0 + (g + 1) * S_STATE]
        cg = big_ref[rows, X0 + XC0 + g * S_STATE:X0 + XC0 + (g + 1) * S_STATE]
        xg_b = big_ref[rows, X0 + g * S_GW:X0 + (g + 1) * S_GW]
        xg = xg_b.astype(F32)
        cbm = cb_groups[g]
        w_heads = []
        for j in range(S_HPG):
            gk = GDT0 + g * S_HPG + j
            ca_b = rep_ref[2 * M_HEADS + g * S_HPG + j, rows, :]
            ca_r = cum_t[gk:gk + 1, :]
            dt_r = act_t[gk:gk + 1, :]
            dec = jnp.exp2(jnp.where(causal, ca_b - ca_r, neg_inf))
            w_heads.append((cbm * dec * dt_r).astype(BF16))
        y_diag = jnp.zeros((L, S_GW), F32)
        for j in range(0, S_HPG, 2):
            y_diag = y_diag + jnp.dot(
                jnp.concatenate([w_heads[j], w_heads[j + 1]], axis=1),
                jnp.concatenate([xg_b * hmask[j], xg_b * hmask[j + 1]], axis=0),
                preferred_element_type=F32)
        e_cum = e_ref[pl.ds(e_rows, L), gcols]
        e_end = e_ref[pl.ds(e_rows + L, L), gcols]
        e_tot = e_ref[pl.ds(e_rows + L - SUBLANE, SUBLANE), gcols][SUBLANE - 1:SUBLANE, :]
        y_off = jnp.dot(cg, st_ref[g].astype(BF16), preferred_element_type=F32) * e_cum
        y = y_diag + y_off + p["sd"][:, gcols] * xg
        yz = y * big_ref[rows, Z0 + g * S_GW:Z0 + (g + 1) * S_GW].astype(F32)
        ms = jnp.mean(yz * yz, axis=1, keepdims=True)
        ys_ref[rows, gcols] = (yz * lax.rsqrt(ms + EPS) * p["sng"][:, gcols]).astype(BF16)
        xw = (xg * e_end).astype(BF16)
        s_new = lax.dot_general(bg, xw, (((0,), (0,)), ((), ())), preferred_element_type=F32)
        st_ref[g] = st_ref[g] * e_tot + s_new


_SMALL_KEYS = ("gbias", "alog", "mng", "sd", "sng", "expand")


def _init_kernel(big_ref, gate_ref, gbias_ref, alog_ref, cw_ref, cb_ref, expand_ref,
                 ct_ref, n_ref, m_ref, st_ref, xh_ref):
    L = CHUNK
    act, cum = _gate_block(gate_ref[...] + gbias_ref[...], alog_ref[...], masked=True)
    ct_ref[...] = jnp.zeros_like(ct_ref)
    n_ref[...] = jnp.zeros_like(n_ref)
    m_ref[...] = jnp.zeros_like(m_ref)
    for h in range(M_HEADS):
        k = big_ref[:, K0 + h * M_DQK:K0 + (h + 1) * M_DQK]
        v = big_ref[:, V0 + h * M_DV:V0 + (h + 1) * M_DV]
        gi, gf = GI0 + h, GF0 + h
        bt_b = _lane_rep(cum, gf)
        _mlstm_state(k, v, bt_b, _lane_rep(act, gi), bt_b[L - 1:L, :], m_ref[h:h + 1, :],
                     ct_ref, n_ref, m_ref, h)

    e_all = _head_expand(act, cum, expand_ref)
    valid = lax.broadcasted_iota(jnp.int32, (L, S_XBC), 0) >= (L - N_META)
    xa = jnp.where(valid, big_ref[:, X0:X0 + S_XBC].astype(F32), 0.0)
    xh_ref[...] = xa[L - SUBLANE:L, :]
    xc = _conv_silu(xa, jnp.zeros((SUBLANE, S_XBC), F32), cw_ref, cb_ref, slice(0, S_XBC))
    for g in range(S_GROUPS):
        gcols = slice(g * S_GW, (g + 1) * S_GW)
        bg = xc[:, XB0 + g * S_STATE:XB0 + (g + 1) * S_STATE].astype(BF16)
        xw = (xc[:, gcols] * e_all[L:2 * L, gcols]).astype(BF16)
        st_ref[g] = lax.dot_general(bg, xw, (((0,), (0,)), ((), ())), preferred_element_type=F32)


_STATE_SHAPES = (
    (M_HEADS, M_DQK, M_DV),
    (SUBLANE, M_DQK),
    (SUBLANE, LANE),
    (S_GROUPS, S_STATE, S_GW),
    (SUBLANE, S_XBC),
)


def _init_state(big_m, gate_m, consts):
    return pl.pallas_call(
        _init_kernel,
        grid=(1,),
        in_specs=[_const_spec(big_m.shape), _const_spec(gate_m.shape)] + [_const_spec(a.shape) for a in consts],
        out_specs=[_const_spec(s) for s in _STATE_SHAPES],
        out_shape=[jax.ShapeDtypeStruct(s, F32) for s in _STATE_SHAPES],
        compiler_params=pltpu.CompilerParams(
            dimension_semantics=("arbitrary",), vmem_limit_bytes=VMEM_LIMIT),
        name="init_state",
    )(big_m, gate_m, *consts)


def _mixer_kernel(big_ref, gate_ref, x_ref, *refs):
    ns = len(_SMALL_KEYS)
    p = dict(zip(_SMALL_KEYS, refs[:ns]))
    ct0, n0, m0, st0 = refs[ns:ns + 4]
    mproj_ref, sproj_ref, wout_ref = refs[ns + 4:ns + 7]
    h1_ref = refs[ns + 7]
    ct_ref, n_ref, m_ref, st_ref, hm_ref, ys_ref, mg_ref = refs[ns + 8:ns + 15]
    gs = dict(zip(("rep", "act_t", "cum_t", "expanded"), refs[ns + 15:]))

    @pl.when(pl.program_id(1) == 0)
    def _():
        ct_ref[...] = ct0[...]
        n_ref[...] = n0[...]
        m_ref[...] = m0[...]
        st_ref[...] = st0[...]

    for c in range(CPT):
        rs = slice(c * CHUNK, (c + 1) * CHUNK)
        act, cum = _gate_block(gate_ref[rs, :] + p["gbias"][...], p["alog"][...], masked=False)
        for h in range(M_HEADS):
            gs["rep"][h, rs, :] = _lane_rep(cum, GF0 + h)
            gs["rep"][M_HEADS + h, rs, :] = _lane_rep(act, GI0 + h)
        for j in range(S_HEADS):
            gs["rep"][2 * M_HEADS + j, rs, :] = _lane_rep(cum, GDT0 + j)
        gs["act_t"][rs, :] = act.T
        gs["cum_t"][rs, :] = cum.T
        gs["expanded"][2 * c * CHUNK:2 * (c + 1) * CHUNK, :] = _head_expand(act, cum, p["expand"])

    lane_g = lax.broadcasted_iota(jnp.int32, (CHUNK, S_GW), 1)
    hmask = [((lane_g >= j * S_HEADDIM) & (lane_g < (j + 1) * S_HEADDIM)).astype(F32).astype(BF16)
             for j in range(S_HPG)]

    def body(c, carry):
        r0 = pl.multiple_of(c * CHUNK, CHUNK)
        _chunk_step(r0, big_ref, gs, p, ct_ref, n_ref, m_ref, st_ref, hm_ref, ys_ref, hmask)
        return carry

    for c in range(CPT):
        body(c, 0)

    for nb in range(D_MODEL // FB):
        cols = slice(nb * FB, (nb + 1) * FB)
        br_a = jnp.dot(hm_ref[...], mproj_ref[:, cols], preferred_element_type=F32)
        br_b = jnp.dot(ys_ref[...], sproj_ref[:, cols], preferred_element_type=F32)
        ga = big_ref[:, GA0 + nb * FB:GA0 + (nb + 1) * FB].astype(F32)
        gb = big_ref[:, GB0 + nb * FB:GB0 + (nb + 1) * FB].astype(F32)
        mg_ref[:, cols] = (ga * br_a + gb * br_b).astype(BF16)
    for nb in range(D_MODEL // FB):
        cols = slice(nb * FB, (nb + 1) * FB)
        h1_ref[:, cols] = x_ref[:, cols] + jnp.dot(mg_ref[...], wout_ref[:, cols], preferred_element_type=F32)


def _mixer(big, gate, x2, small, init, mproj, sproj, wout, batch):
    n = x2.shape[0]
    tm = MIX_TM
    tiles = n // batch // tm
    row_map = lambda b, j: (b * tiles + j, 0)
    state_shapes = _STATE_SHAPES[:4]
    return pl.pallas_call(
        _mixer_kernel,
        grid=(batch, tiles),
        in_specs=[
            pl.BlockSpec((tm, BIG_W), row_map),
            pl.BlockSpec((tm, GATE_W), row_map),
            pl.BlockSpec((tm, D_MODEL), row_map),
        ] + [_const_spec(s.shape) for s in small]
          + [_const_spec(s) for s in state_shapes]
          + [_const_spec((D_MODEL, D_MODEL))] * 3,
        out_specs=pl.BlockSpec((tm, D_MODEL), row_map),
        out_shape=jax.ShapeDtypeStruct((n, D_MODEL), F32),
        scratch_shapes=[pltpu.VMEM(s, F32) for s in state_shapes] + [
            pltpu.VMEM((tm, M_V), BF16),
            pltpu.VMEM((tm, S_INNER), BF16),
            pltpu.VMEM((tm, D_MODEL), BF16),
            pltpu.VMEM((2 * M_HEADS + S_HEADS, tm, LANE), F32),
            pltpu.VMEM((tm, GATE_W), F32),
            pltpu.VMEM((tm, GATE_W), F32),
            pltpu.VMEM((2 * tm, S_INNER), F32),
        ],
        compiler_params=pltpu.CompilerParams(
            dimension_semantics=("arbitrary", "arbitrary"), vmem_limit_bytes=VMEM_LIMIT),
        name="mixer",
    )(big, gate, x2, *small, *init[:4], mproj, sproj, wout)


def _ffn_kernel(h_ref, g2_ref, gf_ref, w1_ref, w2_ref, o_ref, u_ref, hid_ref):
    h = h_ref[...]
    ms = jnp.mean(h * h, axis=-1, keepdims=True)
    u_ref[...] = (h * lax.rsqrt(ms + EPS) * g2_ref[...]).astype(BF16)
    for jb in range(D_FF // FB):
        gate = jnp.dot(u_ref[...], w1_ref[:, jb * FB:(jb + 1) * FB], preferred_element_type=F32)
        up = jnp.dot(u_ref[...], w1_ref[:, D_FF + jb * FB:D_FF + (jb + 1) * FB], preferred_element_type=F32)
        hid_ref[:, jb * FB:(jb + 1) * FB] = (gate * jax.nn.sigmoid(gate) * up).astype(BF16)
    for nb in range(D_MODEL // FB):
        cols = slice(nb * FB, (nb + 1) * FB)
        o_ref[:, cols] = h_ref[:, cols] + jnp.dot(hid_ref[...], w2_ref[:, cols], preferred_element_type=F32)
    h2 = o_ref[...]
    ms2 = jnp.mean(h2 * h2, axis=-1, keepdims=True)
    o_ref[...] = h2 * lax.rsqrt(ms2 + EPS) * gf_ref[...]


def _ffn(h1, g2, gf, w1, w2):
    n = h1.shape[0]
    return pl.pallas_call(
        _ffn_kernel,
        grid=(n // TM,),
        in_specs=[
            pl.BlockSpec((TM, D_MODEL), lambda i: (i, 0)),
            _const_spec((1, D_MODEL)),
            _const_spec((1, D_MODEL)),
            _const_spec((D_MODEL, 2 * D_FF)),
            _const_spec((D_FF, D_MODEL)),
        ],
        out_specs=pl.BlockSpec((TM, D_MODEL), lambda i: (i, 0)),
        out_shape=jax.ShapeDtypeStruct((n, D_MODEL), F32),
        scratch_shapes=[pltpu.VMEM((TM, D_MODEL), BF16), pltpu.VMEM((TM, D_FF), BF16)],
        compiler_params=pltpu.CompilerParams(
            dimension_semantics=("arbitrary",), vmem_limit_bytes=VMEM_LIMIT),
        name="ffn",
    )(h1, g2, gf, w1, w2)


def _pad_lanes(row, width):
    return jnp.pad(row, ((0, 0), (0, width - row.shape[1])))


def kernel(x, meta, norm1_g, w_in, m_igate_b, m_fgate_b, m_norm_g, m_proj, s_conv_w, s_conv_b,
           s_dt_bias, s_A_log, s_D, s_norm_g, s_proj, w_out, norm2_g, w_ffn_in, w_ffn_out, norm_f_g):
    bsz, seq, dm = x.shape
    assert dm == D_MODEL and seq % TM == 0 and w_in.shape[0] == 1
    w_all = _pack_w_in(w_in[0].T)

    gbias = _pad_lanes(jnp.concatenate([m_igate_b[0], m_fgate_b[0], s_dt_bias[0]])[None].astype(F32), GATE_W)
    alog = _pad_lanes(jnp.concatenate([jnp.zeros((GDT0,), F32), s_A_log[0].astype(F32)])[None], GATE_W)
    conv_w = jnp.pad(s_conv_w[0].astype(F32), ((0, SUBLANE - S_CONV), (0, 0)))
    conv_b = s_conv_b[0][None].astype(F32)
    head_of_lane = jnp.arange(S_INNER) // S_HEADDIM
    expand1 = (jnp.arange(GATE_W)[:, None] == GDT0 + head_of_lane[None, :]).astype(BF16)
    expand = jnp.concatenate([expand1, expand1], axis=0)
    small = (
        gbias, alog,
        m_norm_g[0].reshape(1, M_V).astype(F32),
        jnp.repeat(s_D[0].astype(F32), S_HEADDIM)[None],
        s_norm_g[0].reshape(1, S_INNER).astype(F32),
        expand,
    )
    g1 = norm1_g[0][None].astype(F32)

    meta_rows = jnp.concatenate([jnp.zeros((CHUNK - N_META, D_MODEL), x.dtype), meta.astype(x.dtype)], axis=0)
    big_m, gate_m = _inproj(meta_rows, g1, w_all, CHUNK)
    init = _init_state(big_m, gate_m, (gbias, alog, conv_w, conv_b, expand))

    x2 = x.reshape(bsz * seq, dm)
    big, gate = _inproj(x2, g1, w_all, TM, conv=(init[4], conv_w, conv_b), batch=bsz)
    h1 = _mixer(big, gate, x2, small, init, m_proj[0].astype(BF16), s_proj[0].astype(BF16),
                w_out[0].astype(BF16), bsz)
    out = _ffn(h1, norm2_g[0][None].astype(F32), norm_f_g[None].astype(F32),
               w_ffn_in[0].astype(BF16), w_ffn_out[0].astype(BF16))
    return out.reshape(bsz, seq, dm)
```

```python
import functools
import math

import jax
import jax.numpy as jnp
from jax import lax
from jax.experimental import pallas as pl
from jax.experimental.pallas import tpu as pltpu

F32 = jnp.float32
BF16 = jnp.bfloat16
HIGHEST = lax.Precision.HIGHEST
LOG2E = math.log2(math.e)

D_MODEL = 1024
N_META = 16
CHUNK = 128
EPS = 1e-6

M_HEADS = 4
M_DQK = 128
M_DV = 256
M_QK = M_HEADS * M_DQK
M_V = M_HEADS * M_DV
GATE_CAP = 15.0

S_HEADDIM = 64
S_INNER = D_MODEL
S_HEADS = S_INNER // S_HEADDIM
S_GROUPS = 4
S_HPG = S_HEADS // S_GROUPS
S_STATE = 128
S_CONV = 4
S_GW = S_HPG * S_HEADDIM
S_XBC = S_INNER + 2 * S_GROUPS * S_STATE
XB0 = S_INNER
XC0 = S_INNER + S_GROUPS * S_STATE

D_FF = 2816

Q0 = 0
K0 = Q0 + M_QK
V0 = K0 + M_QK
O0 = V0 + M_V
Z0 = O0 + M_V
X0 = Z0 + S_INNER
GA0 = X0 + S_XBC
GB0 = GA0 + D_MODEL
BIG_W = GB0 + D_MODEL
GATE_W = 128
GI0, GF0, GDT0, GEND = 0, M_HEADS, 2 * M_HEADS, 2 * M_HEADS + S_HEADS

LANE = 128
SUBLANE = 8
TM = 512
MIX_TM = 512
FFN_TM = 1024
CPT = MIX_TM // CHUNK
NB = 512
FB = 256
VMEM_LIMIT = 56 * 1024 * 1024


def _softplus(x):
    return jnp.maximum(x, 0.0) + jnp.log1p(jnp.exp(-jnp.abs(x)))


def _const_spec(shape):
    nd = len(shape)
    return pl.BlockSpec(shape, lambda *_: (0,) * nd, pipeline_mode=pl.Buffered(1))


def _conv_silu(xa, hist, cw_ref, cb_ref, cols):
    row8 = lax.broadcasted_iota(jnp.int32, (SUBLANE, xa.shape[1]), 0)
    acc = cb_ref[:, cols] + xa * cw_ref[S_CONV - 1:S_CONV, cols]
    for sh in range(1, S_CONV):
        rolled = pltpu.roll(xa, sh, axis=0)
        top = jnp.where(row8 < sh, pltpu.roll(hist, sh, axis=0), rolled[0:SUBLANE])
        shifted = jnp.concatenate([top, rolled[SUBLANE:]], axis=0)
        acc = acc + shifted * cw_ref[S_CONV - 1 - sh:S_CONV - sh, cols]
    return acc * jax.nn.sigmoid(acc)


_IN_SIZES = (M_QK, M_QK, M_V, M_V, M_HEADS, M_HEADS, S_INNER, S_XBC, S_HEADS, D_MODEL, D_MODEL)
_IN_OFFS = tuple(sum(_IN_SIZES[:i]) for i in range(len(_IN_SIZES) + 1))
IN_WIDTH = _IN_OFFS[-1]
_PACK_SEGMENTS = (
    (_IN_OFFS[0], _IN_OFFS[4], Q0),
    (_IN_OFFS[6], _IN_OFFS[7], Z0),
    (_IN_OFFS[7], _IN_OFFS[8], X0),
    (_IN_OFFS[9], _IN_OFFS[10], GA0),
    (_IN_OFFS[10], _IN_OFFS[11], GB0),
)
PACK_ROWS = 512


def _pack_kernel(wt_ref, o_ref):
    for s0, s1, d0 in _PACK_SEGMENTS:
        for r in range(s0, s1, PACK_ROWS):
            o_ref[:, d0 + r - s0:d0 + r - s0 + PACK_ROWS] = wt_ref[r:r + PACK_ROWS, :].T.astype(BF16)
    kb = wt_ref.shape[1]
    gates = jnp.concatenate(
        [wt_ref[_IN_OFFS[4]:_IN_OFFS[6], :], wt_ref[_IN_OFFS[8]:_IN_OFFS[9], :],
         jnp.zeros((GATE_W - GEND, kb), F32)], axis=0)
    o_ref[:, BIG_W:BIG_W + GATE_W] = gates.T.astype(BF16)


def _pack_w_in(wt):
    kb = LANE
    return pl.pallas_call(
        _pack_kernel,
        grid=(D_MODEL // kb,),
        in_specs=[pl.BlockSpec((IN_WIDTH, kb), lambda i: (0, i))],
        out_specs=pl.BlockSpec((kb, BIG_W + GATE_W), lambda i: (i, 0)),
        out_shape=jax.ShapeDtypeStruct((D_MODEL, BIG_W + GATE_W), BF16),
        compiler_params=pltpu.CompilerParams(
            dimension_semantics=("arbitrary",), vmem_limit_bytes=VMEM_LIMIT),
        name="pack_w_in",
    )(wt)


def _inproj_kernel(x_ref, g_ref, w_ref, *refs, activate):
    if activate:
        xh0_ref, cw_ref, cb_ref, big_ref, gate_ref, u_ref, cs_ref = refs

        @pl.when(pl.program_id(1) == 0)
        def _():
            for s in range(S_XBC // LANE):
                cs_ref[s, 0:SUBLANE, :] = xh0_ref[:, s * LANE:(s + 1) * LANE]
    else:
        big_ref, gate_ref, u_ref = refs
    tm = x_ref.shape[0]
    xf = x_ref[...]
    ms = jnp.mean(xf * xf, axis=-1, keepdims=True)
    u_ref[...] = (xf * lax.rsqrt(ms + EPS) * g_ref[...]).astype(BF16)
    for n in range(BIG_W // NB):
        c0 = n * NB
        acc = jnp.dot(u_ref[...], w_ref[:, c0:c0 + NB], preferred_element_type=F32)
        if c0 < K0:
            acc = acc * (M_DQK ** -0.5)
        if activate:
            if O0 <= c0 < Z0 or c0 >= GA0:
                acc = jax.nn.sigmoid(acc)
            elif Z0 <= c0 < X0:
                acc = acc * jax.nn.sigmoid(acc)
            elif X0 <= c0 < GA0:
                for s in range(NB // LANE):
                    slab = (c0 - X0) // LANE + s
                    cols = slice((c0 - X0) + s * LANE, (c0 - X0) + (s + 1) * LANE)
                    cs_ref[slab, SUBLANE:SUBLANE + tm, :] = acc[:, s * LANE:(s + 1) * LANE]
                    half = 0.5 * cb_ref[:, cols]
                    for tap in range(S_CONV):
                        back = S_CONV - 1 - tap
                        half = half + (cs_ref[slab, pl.ds(SUBLANE - back, tm, stride=1), :]
                                       * (0.5 * cw_ref[tap:tap + 1, cols]))
                    cs_ref[slab, 0:SUBLANE, :] = cs_ref[slab, tm:tm + SUBLANE, :]
                    big_ref[:, c0 + s * LANE:c0 + (s + 1) * LANE] = (half + half * jnp.tanh(half)).astype(BF16)
                continue
        big_ref[:, c0:c0 + NB] = acc.astype(BF16)
    gate_ref[...] = jnp.dot(u_ref[...], w_ref[:, BIG_W:BIG_W + GATE_W], preferred_element_type=F32)


def _inproj(rows, g, w_all, tm, conv=None, batch=1):
    n = rows.shape[0]
    tiles = n // batch // tm
    row_map = lambda b, j: (b * tiles + j, 0)
    activate = conv is not None
    extra = list(conv) if activate else []
    scratch = [pltpu.VMEM((tm, D_MODEL), BF16)]
    if activate:
        scratch.append(pltpu.VMEM((S_XBC // LANE, SUBLANE + tm, LANE), F32))
    return pl.pallas_call(
        functools.partial(_inproj_kernel, activate=activate),
        grid=(batch, tiles),
        in_specs=[
            pl.BlockSpec((tm, D_MODEL), row_map),
            _const_spec((1, D_MODEL)),
            _const_spec((D_MODEL, BIG_W + GATE_W)),
        ] + [_const_spec(a.shape) for a in extra],
        out_specs=[
            pl.BlockSpec((tm, BIG_W), row_map),
            pl.BlockSpec((tm, GATE_W), row_map),
        ],
        out_shape=[
            jax.ShapeDtypeStruct((n, BIG_W), BF16),
            jax.ShapeDtypeStruct((n, GATE_W), F32),
        ],
        scratch_shapes=scratch,
        compiler_params=pltpu.CompilerParams(
            dimension_semantics=("arbitrary", "arbitrary"), vmem_limit_bytes=VMEM_LIMIT),
        name="inproj_act" if activate else "inproj_raw",
    )(rows, g, w_all, *extra)


def _gate_block(pre, alog_row, masked, tril3):
    L = pre.shape[0]
    lane = lax.broadcasted_iota(jnp.int32, (L, GATE_W), 1)
    sc = GATE_CAP * jnp.tanh(pre / GATE_CAP)
    i_log = sc * LOG2E
    f_log = -_softplus(-sc) * LOG2E
    dt = _softplus(pre)
    if masked:
        valid = lax.broadcasted_iota(jnp.int32, (L, GATE_W), 0) >= (L - N_META)
        i_log = jnp.where(valid, i_log, -jnp.inf)
        f_log = jnp.where(valid, f_log, 0.0)
        dt = jnp.where(valid, dt, 0.0)
    act = jnp.where(lane < GF0, i_log, jnp.where(lane < GDT0, f_log, jnp.where(lane < GEND, dt, 0.0)))
    a_row = -jnp.exp(alog_row) * LOG2E
    cs = jnp.where(lane < GF0, 0.0, jnp.where(lane < GDT0, act, jnp.where(lane < GEND, act * a_row, 0.0)))
    hi = cs.astype(BF16)
    rem = cs - hi.astype(F32)
    mid = rem.astype(BF16)
    lo = (rem - mid.astype(F32)).astype(BF16)
    cum = jnp.dot(tril3, jnp.concatenate([hi, mid, lo], axis=0), preferred_element_type=F32)
    return act, cum


def _tril3(L):
    assert L & (L - 1) == 0
    row_i = lax.broadcasted_iota(jnp.int32, (L, 3 * L), 0)
    col_i = lax.broadcasted_iota(jnp.int32, (L, 3 * L), 1)
    return (jnp.bitwise_and(col_i, L - 1) <= row_i).astype(F32).astype(BF16)


def _lane_rep(x, col):
    return jnp.broadcast_to(x[:, col:col + 1], (x.shape[0], LANE))


def _mlstm_state(k, v, bt_b, it_b, b_tot, m_prev, ct_ref, n_ref, m_ref, h):
    w_end = b_tot - bt_b + it_b
    m_loc = jnp.max(w_end, axis=0, keepdims=True)
    wgt = jnp.exp2(w_end - m_loc)
    kw = k.astype(F32) * wgt
    s_loc = lax.dot_general(kw.astype(BF16), v, (((0,), (0,)), ((), ())), preferred_element_type=F32)
    n_loc = jnp.sum(kw, axis=0, keepdims=True)
    m_new = jnp.maximum(b_tot + m_prev, m_loc)
    a_dec = jnp.exp2(b_tot + m_prev - m_new)
    s_dec = jnp.exp2(m_loc - m_new)
    ct_ref[h] = (jnp.concatenate([a_dec, a_dec], axis=1) * ct_ref[h]
                 + jnp.concatenate([s_dec, s_dec], axis=1) * s_loc)
    n_ref[h:h + 1, :] = a_dec * n_ref[h:h + 1, :] + s_dec * n_loc
    m_ref[h:h + 1, :] = m_new


def _head_expand(act, cum, expand_ref):
    L = act.shape[0]
    lane = lax.broadcasted_iota(jnp.int32, (L, GATE_W), 1)
    is_dt = (lane >= GDT0) & (lane < GEND)
    tot = cum[L - 1:L, :]
    e_cum = jnp.where(is_dt, jnp.exp2(cum), 0.0)
    e_end = jnp.where(is_dt, jnp.exp2(tot - cum) * act, 0.0)

    def split(x):
        hi = x.astype(BF16)
        lo = (x - hi.astype(F32)).astype(BF16)
        return jnp.concatenate([hi, lo], axis=1)

    lhs = jnp.concatenate([split(e_cum), split(e_end)], axis=0)
    return jnp.dot(lhs, expand_ref[...], preferred_element_type=F32)


def _chunk_step(r0, big_ref, gs, p, ct_ref, n_ref, m_ref, st_ref, hm_ref, ys_ref, hmask):
    L = CHUNK
    rows = pl.ds(r0, L)
    row_i = lax.broadcasted_iota(jnp.int32, (L, L), 0)
    col_i = lax.broadcasted_iota(jnp.int32, (L, L), 1)
    causal = col_i <= row_i
    neg_inf = -jnp.inf

    rep_ref = gs["rep"]
    act_t = gs["act_t"][rows, :]
    cum_t = gs["cum_t"][rows, :]
    e_rows = pl.multiple_of(2 * r0, 2 * L)

    zero_blk = jnp.zeros((L, LANE), BF16)

    def paired_nt(lhs_pair, rhs0, rhs1):
        rhs = jnp.concatenate([jnp.concatenate([rhs0, zero_blk], axis=1),
                               jnp.concatenate([zero_blk, rhs1], axis=1)], axis=0)
        out = lax.dot_general(lhs_pair, rhs, (((1,), (1,)), ((), ())), preferred_element_type=F32)
        return out[:, :LANE], out[:, LANE:]

    qk_heads = []
    for pr in range(M_HEADS // 2):
        qk_heads += paired_nt(
            big_ref[rows, Q0 + 2 * pr * M_DQK:Q0 + 2 * (pr + 1) * M_DQK],
            big_ref[rows, K0 + 2 * pr * M_DQK:K0 + (2 * pr + 1) * M_DQK],
            big_ref[rows, K0 + (2 * pr + 1) * M_DQK:K0 + 2 * (pr + 1) * M_DQK])

    for h in range(M_HEADS):
        q = big_ref[rows, Q0 + h * M_DQK:Q0 + (h + 1) * M_DQK]
        k = big_ref[rows, K0 + h * M_DQK:K0 + (h + 1) * M_DQK]
        v = big_ref[rows, V0 + h * M_DV:V0 + (h + 1) * M_DV]
        gi, gf = GI0 + h, GF0 + h
        bt_b = rep_ref[h, rows, :]
        it_b = rep_ref[M_HEADS + h, rows, :]
        bt_r = cum_t[gf:gf + 1, :]
        it_r = act_t[gi:gi + 1, :]
        b_tot = bt_b[L - 1:L, :]
        m_prev = m_ref[h:h + 1, :]
        qk = qk_heads[h]
        d_log = jnp.where(causal, bt_b - bt_r + it_r, neg_inf)
        mx = jnp.max(d_log, axis=1, keepdims=True)
        inter_log = bt_b + m_prev
        m_t = jnp.maximum(inter_log, mx)
        w_ts = jnp.exp2(d_log - m_t) * qk
        inter = jnp.exp2(inter_log - m_t)
        q_f = q.astype(F32)
        lhs = jnp.concatenate([w_ts.astype(BF16), (q_f * inter).astype(BF16)], axis=1)
        rhs = jnp.concatenate([v, ct_ref[h].astype(BF16)], axis=0)
        num = jnp.dot(lhs, rhs, preferred_element_type=F32)
        q_n = jnp.sum(q_f * n_ref[h:h + 1, :], axis=1, keepdims=True)
        den = jnp.sum(w_ts, axis=1, keepdims=True) + inter * q_n
        denom = jnp.maximum(jnp.abs(den), jnp.exp2(-m_t))
        ms = jnp.mean(num * num, axis=1, keepdims=True)
        rs = lax.rsqrt(ms + EPS * (denom * denom))
        y = num * jnp.concatenate([rs, rs], axis=1) * p["mng"][:, h * M_DV:(h + 1) * M_DV]
        o_gate = big_ref[rows, O0 + h * M_DV:O0 + (h + 1) * M_DV]
        hm_ref[rows, h * M_DV:(h + 1) * M_DV] = y.astype(BF16) * o_gate
        _mlstm_state(k, v, bt_b, it_b, b_tot, m_prev, ct_ref, n_ref, m_ref, h)

    e_ref = gs["expanded"]
    cb_groups = []
    for pr in range(S_GROUPS // 2):
        cb_groups += paired_nt(
            big_ref[rows, X0 + XC0 + 2 * pr * S_STATE:X0 + XC0 + 2 * (pr + 1) * S_STATE],
            big_ref[rows, X0 + XB0 + 2 * pr * S_STATE:X0 + XB0 + (2 * pr + 1) * S_STATE],
            big_ref[rows, X0 + XB0 + (2 * pr + 1) * S_STATE:X0 + XB0 + 2 * (pr + 1) * S_STATE])
    for g in range(S_GROUPS):
        gcols = slice(g * S_GW, (g + 1) * S_GW)
        bg = big_ref[rows, X0 + XB0 + g * S_STATE:X0 + XB0 + (g + 1) * S_STATE]
        cg = big_ref[rows, X0 + XC0 + g * S_STATE:X0 + XC0 + (g + 1) * S_STATE]
        xg_b = big_ref[rows, X0 + g * S_GW:X0 + (g + 1) * S_GW]
        xg = xg_b.astype(F32)
        cbm = cb_groups[g]
        w_heads = []
        for j in range(S_HPG):
            gk = GDT0 + g * S_HPG + j
            ca_b = rep_ref[2 * M_HEADS + g * S_HPG + j, rows, :]
            ca_r = cum_t[gk:gk + 1, :]
            dt_r = act_t[gk:gk + 1, :]
            dec = jnp.exp2(jnp.where(causal, ca_b - ca_r, neg_inf))
            w_heads.append((cbm * dec * dt_r).astype(BF16))
        y_diag = jnp.zeros((L, S_GW), F32)
        for j in range(0, S_HPG, 2):
            y_diag = y_diag + jnp.dot(
                jnp.concatenate([w_heads[j], w_heads[j + 1]], axis=1),
                jnp.concatenate([xg_b * hmask[j], xg_b * hmask[j + 1]], axis=0),
                preferred_element_type=F32)
        e_cum = e_ref[pl.ds(e_rows, L), gcols]
        e_end = e_ref[pl.ds(e_rows + L, L), gcols]
        e_tot = e_ref[pl.ds(e_rows + L - SUBLANE, SUBLANE), gcols][SUBLANE - 1:SUBLANE, :]
        y_off = jnp.dot(cg, st_ref[g].astype(BF16), preferred_element_type=F32) * e_cum
        y = y_diag + y_off + p["sd"][:, gcols] * xg
        yz = y * big_ref[rows, Z0 + g * S_GW:Z0 + (g + 1) * S_GW].astype(F32)
        ms = jnp.mean(yz * yz, axis=1, keepdims=True)
        ys_ref[rows, gcols] = (yz * lax.rsqrt(ms + EPS) * p["sng"][:, gcols]).astype(BF16)
        xw = (xg * e_end).astype(BF16)
        s_new = lax.dot_general(bg, xw, (((0,), (0,)), ((), ())), preferred_element_type=F32)
        st_ref[g] = st_ref[g] * e_tot + s_new


_SMALL_KEYS = ("gbias", "alog", "mng", "sd", "sng", "expand")


def _init_kernel(big_ref, gate_ref, gbias_ref, alog_ref, cw_ref, cb_ref, expand_ref,
                 ct_ref, n_ref, m_ref, st_ref, xh_ref):
    L = CHUNK
    act, cum = _gate_block(gate_ref[...] + gbias_ref[...], alog_ref[...], True, _tril3(L))
    ct_ref[...] = jnp.zeros_like(ct_ref)
    n_ref[...] = jnp.zeros_like(n_ref)
    m_ref[...] = jnp.zeros_like(m_ref)
    for h in range(M_HEADS):
        k = big_ref[:, K0 + h * M_DQK:K0 + (h + 1) * M_DQK]
        v = big_ref[:, V0 + h * M_DV:V0 + (h + 1) * M_DV]
        gi, gf = GI0 + h, GF0 + h
        bt_b = _lane_rep(cum, gf)
        _mlstm_state(k, v, bt_b, _lane_rep(act, gi), bt_b[L - 1:L, :], m_ref[h:h + 1, :],
                     ct_ref, n_ref, m_ref, h)

    e_all = _head_expand(act, cum, expand_ref)
    valid = lax.broadcasted_iota(jnp.int32, (L, S_XBC), 0) >= (L - N_META)
    xa = jnp.where(valid, big_ref[:, X0:X0 + S_XBC].astype(F32), 0.0)
    xh_ref[...] = xa[L - SUBLANE:L, :]
    xc = _conv_silu(xa, jnp.zeros((SUBLANE, S_XBC), F32), cw_ref, cb_ref, slice(0, S_XBC))
    for g in range(S_GROUPS):
        gcols = slice(g * S_GW, (g + 1) * S_GW)
        bg = xc[:, XB0 + g * S_STATE:XB0 + (g + 1) * S_STATE].astype(BF16)
        xw = (xc[:, gcols] * e_all[L:2 * L, gcols]).astype(BF16)
        st_ref[g] = lax.dot_general(bg, xw, (((0,), (0,)), ((), ())), preferred_element_type=F32)


_STATE_SHAPES = (
    (M_HEADS, M_DQK, M_DV),
    (SUBLANE, M_DQK),
    (SUBLANE, LANE),
    (S_GROUPS, S_STATE, S_GW),
    (SUBLANE, S_XBC),
)


def _init_state(big_m, gate_m, consts):
    return pl.pallas_call(
        _init_kernel,
        grid=(1,),
        in_specs=[_const_spec(big_m.shape), _const_spec(gate_m.shape)] + [_const_spec(a.shape) for a in consts],
        out_specs=[_const_spec(s) for s in _STATE_SHAPES],
        out_shape=[jax.ShapeDtypeStruct(s, F32) for s in _STATE_SHAPES],
        compiler_params=pltpu.CompilerParams(
            dimension_semantics=("arbitrary",), vmem_limit_bytes=VMEM_LIMIT),
        name="init_state",
    )(big_m, gate_m, *consts)


def _mixer_kernel(big_ref, gate_ref, x_ref, *refs):
    ns = len(_SMALL_KEYS)
    p = dict(zip(_SMALL_KEYS, refs[:ns]))
    ct0, n0, m0, st0 = refs[ns:ns + 4]
    mproj_ref, sproj_ref, wout_ref = refs[ns + 4:ns + 7]
    h1_ref = refs[ns + 7]
    ct_ref, n_ref, m_ref, st_ref, hm_ref, ys_ref, mg_ref = refs[ns + 8:ns + 15]
    gs = dict(zip(("rep", "act_t", "cum_t", "expanded"), refs[ns + 15:]))

    @pl.when(pl.program_id(1) == 0)
    def _():
        ct_ref[...] = ct0[...]
        n_ref[...] = n0[...]
        m_ref[...] = m0[...]
        st_ref[...] = st0[...]

    tril3 = _tril3(CHUNK)
    for c in range(CPT):
        rs = slice(c * CHUNK, (c + 1) * CHUNK)
        act, cum = _gate_block(gate_ref[rs, :] + p["gbias"][...], p["alog"][...], False, tril3)
        for h in range(M_HEADS):
            gs["rep"][h, rs, :] = _lane_rep(cum, GF0 + h)
            gs["rep"][M_HEADS + h, rs, :] = _lane_rep(act, GI0 + h)
        for j in range(S_HEADS):
            gs["rep"][2 * M_HEADS + j, rs, :] = _lane_rep(cum, GDT0 + j)
        gs["act_t"][rs, :] = act.T
        gs["cum_t"][rs, :] = cum.T
        gs["expanded"][2 * c * CHUNK:2 * (c + 1) * CHUNK, :] = _head_expand(act, cum, p["expand"])

    lane_g = lax.broadcasted_iota(jnp.int32, (CHUNK, S_GW), 1)
    hmask = [((lane_g >= j * S_HEADDIM) & (lane_g < (j + 1) * S_HEADDIM)).astype(F32).astype(BF16)
             for j in range(S_HPG)]

    def body(c, carry):
        r0 = pl.multiple_of(c * CHUNK, CHUNK)
        _chunk_step(r0, big_ref, gs, p, ct_ref, n_ref, m_ref, st_ref, hm_ref, ys_ref, hmask)
        return carry

    for c in range(CPT):
        body(c, 0)

    for nb in range(D_MODEL // FB):
        cols = slice(nb * FB, (nb + 1) * FB)
        br_a = jnp.dot(hm_ref[...], mproj_ref[:, cols], preferred_element_type=F32)
        br_b = jnp.dot(ys_ref[...], sproj_ref[:, cols], preferred_element_type=F32)
        ga = big_ref[:, GA0 + nb * FB:GA0 + (nb + 1) * FB].astype(F32)
        gb = big_ref[:, GB0 + nb * FB:GB0 + (nb + 1) * FB].astype(F32)
        mg_ref[:, cols] = (ga * br_a + gb * br_b).astype(BF16)
    for nb in range(D_MODEL // FB):
        cols = slice(nb * FB, (nb + 1) * FB)
        h1_ref[:, cols] = x_ref[:, cols] + jnp.dot(mg_ref[...], wout_ref[:, cols], preferred_element_type=F32)


def _mixer(big, gate, x2, small, init, mproj, sproj, wout, batch):
    n = x2.shape[0]
    tm = MIX_TM
    tiles = n // batch // tm
    row_map = lambda b, j: (b * tiles + j, 0)
    state_shapes = _STATE_SHAPES[:4]
    return pl.pallas_call(
        _mixer_kernel,
        grid=(batch, tiles),
        in_specs=[
            pl.BlockSpec((tm, BIG_W), row_map),
            pl.BlockSpec((tm, GATE_W), row_map),
            pl.BlockSpec((tm, D_MODEL), row_map),
        ] + [_const_spec(s.shape) for s in small]
          + [_const_spec(s) for s in state_shapes]
          + [_const_spec((D_MODEL, D_MODEL))] * 3,
        out_specs=pl.BlockSpec((tm, D_MODEL), row_map),
        out_shape=jax.ShapeDtypeStruct((n, D_MODEL), F32),
        scratch_shapes=[pltpu.VMEM(s, F32) for s in state_shapes] + [
            pltpu.VMEM((tm, M_V), BF16),
            pltpu.VMEM((tm, S_INNER), BF16),
            pltpu.VMEM((tm, D_MODEL), BF16),
            pltpu.VMEM((2 * M_HEADS + S_HEADS, tm, LANE), F32),
            pltpu.VMEM((tm, GATE_W), F32),
            pltpu.VMEM((tm, GATE_W), F32),
            pltpu.VMEM((2 * tm, S_INNER), F32),
        ],
        compiler_params=pltpu.CompilerParams(
            dimension_semantics=("arbitrary", "arbitrary"), vmem_limit_bytes=VMEM_LIMIT),
        name="mixer",
    )(big, gate, x2, *small, *init[:4], mproj, sproj, wout)


def _ffn_kernel(h_ref, g2_ref, gf_ref, w1_ref, w2_ref, o_ref, u_ref, hid_ref):
    h = h_ref[...]
    ms = jnp.mean(h * h, axis=-1, keepdims=True)
    u_ref[...] = (h * lax.rsqrt(ms + EPS) * g2_ref[...]).astype(BF16)
    for jb in range(D_FF // FB):
        gate = jnp.dot(u_ref[...], w1_ref[:, jb * FB:(jb + 1) * FB], preferred_element_type=F32)
        up = jnp.dot(u_ref[...], w1_ref[:, D_FF + jb * FB:D_FF + (jb + 1) * FB], preferred_element_type=F32)
        hid_ref[:, jb * FB:(jb + 1) * FB] = (gate * jax.nn.sigmoid(gate) * up).astype(BF16)
    for nb in range(D_MODEL // FB):
        cols = slice(nb * FB, (nb + 1) * FB)
        o_ref[:, cols] = h_ref[:, cols] + jnp.dot(hid_ref[...], w2_ref[:, cols], preferred_element_type=F32)
    h2 = o_ref[...]
    ms2 = jnp.mean(h2 * h2, axis=-1, keepdims=True)
    o_ref[...] = h2 * lax.rsqrt(ms2 + EPS) * gf_ref[...]


def _ffn(h1, g2, gf, w1, w2):
    n = h1.shape[0]
    tm = FFN_TM
    return pl.pallas_call(
        _ffn_kernel,
        grid=(n // tm,),
        in_specs=[
            pl.BlockSpec((tm, D_MODEL), lambda i: (i, 0)),
            _const_spec((1, D_MODEL)),
            _const_spec((1, D_MODEL)),
            _const_spec((D_MODEL, 2 * D_FF)),
            _const_spec((D_FF, D_MODEL)),
        ],
        out_specs=pl.BlockSpec((tm, D_MODEL), lambda i: (i, 0)),
        out_shape=jax.ShapeDtypeStruct((n, D_MODEL), F32),
        scratch_shapes=[pltpu.VMEM((tm, D_MODEL), BF16), pltpu.VMEM((tm, D_FF), BF16)],
        compiler_params=pltpu.CompilerParams(
            dimension_semantics=("arbitrary",), vmem_limit_bytes=VMEM_LIMIT),
        name="ffn",
    )(h1, g2, gf, w1, w2)


def _pad_lanes(row, width):
    return jnp.pad(row, ((0, 0), (0, width - row.shape[1])))


def kernel(x, meta, norm1_g, w_in, m_igate_b, m_fgate_b, m_norm_g, m_proj, s_conv_w, s_conv_b,
           s_dt_bias, s_A_log, s_D, s_norm_g, s_proj, w_out, norm2_g, w_ffn_in, w_ffn_out, norm_f_g):
    bsz, seq, dm = x.shape
    assert dm == D_MODEL and seq % TM == 0 and w_in.shape[0] == 1
    w_all = _pack_w_in(w_in[0].T)

    gbias = _pad_lanes(jnp.concatenate([m_igate_b[0], m_fgate_b[0], s_dt_bias[0]])[None].astype(F32), GATE_W)
    alog = _pad_lanes(jnp.concatenate([jnp.zeros((GDT0,), F32), s_A_log[0].astype(F32)])[None], GATE_W)
    conv_w = jnp.pad(s_conv_w[0].astype(F32), ((0, SUBLANE - S_CONV), (0, 0)))
    conv_b = s_conv_b[0][None].astype(F32)
    head_of_lane = jnp.arange(S_INNER) // S_HEADDIM
    expand1 = (jnp.arange(GATE_W)[:, None] == GDT0 + head_of_lane[None, :]).astype(BF16)
    expand = jnp.concatenate([expand1, expand1], axis=0)
    small = (
        gbias, alog,
        m_norm_g[0].reshape(1, M_V).astype(F32),
        jnp.repeat(s_D[0].astype(F32), S_HEADDIM)[None],
        s_norm_g[0].reshape(1, S_INNER).astype(F32),
        expand,
    )
    g1 = norm1_g[0][None].astype(F32)

    meta_rows = jnp.concatenate([jnp.zeros((CHUNK - N_META, D_MODEL), x.dtype), meta.astype(x.dtype)], axis=0)
    big_m, gate_m = _inproj(meta_rows, g1, w_all, CHUNK)
    init = _init_state(big_m, gate_m, (gbias, alog, conv_w, conv_b, expand))

    x2 = x.reshape(bsz * seq, dm)
    big, gate = _inproj(x2, g1, w_all, TM, conv=(init[4], conv_w, conv_b), batch=bsz)
    h1 = _mixer(big, gate, x2, small, init, m_proj[0].astype(BF16), s_proj[0].astype(BF16),
                w_out[0].astype(BF16), bsz)
    out = _ffn(h1, norm2_g[0][None].astype(F32), norm_f_g[None].astype(F32),
               w_ffn_in[0].astype(BF16), w_ffn_out[0].astype(BF16))
    return out.reshape(bsz, seq, dm)
```

```python
import functools
import math

import jax
import jax.numpy as jnp
from jax import lax
from jax.experimental import pallas as pl
from jax.experimental.pallas import tpu as pltpu

F32 = jnp.float32
BF16 = jnp.bfloat16
HIGHEST = lax.Precision.HIGHEST
LOG2E = math.log2(math.e)

D_MODEL = 1024
N_META = 16
CHUNK = 128
EPS = 1e-6

M_HEADS = 4
M_DQK = 128
M_DV = 256
M_QK = M_HEADS * M_DQK
M_V = M_HEADS * M_DV
GATE_CAP = 15.0

S_HEADDIM = 64
S_INNER = D_MODEL
S_HEADS = S_INNER // S_HEADDIM
S_GROUPS = 4
S_HPG = S_HEADS // S_GROUPS
S_STATE = 128
S_CONV = 4
S_GW = S_HPG * S_HEADDIM
S_XBC = S_INNER + 2 * S_GROUPS * S_STATE
XB0 = S_INNER
XC0 = S_INNER + S_GROUPS * S_STATE

D_FF = 2816

Q0 = 0
K0 = Q0 + M_QK
V0 = K0 + M_QK
O0 = V0 + M_V
Z0 = O0 + M_V
X0 = Z0 + S_INNER
GA0 = X0 + S_XBC
GB0 = GA0 + D_MODEL
BIG_W = GB0 + D_MODEL
GATE_W = 128
GI0, GF0, GDT0, GEND = 0, M_HEADS, 2 * M_HEADS, 2 * M_HEADS + S_HEADS

LANE = 128
SUBLANE = 8
TM = 512
MIX_TM = 512
FFN_TM = 1024
CPT = MIX_TM // CHUNK
NB = 512
FB = 256
VMEM_LIMIT = 56 * 1024 * 1024


def _softplus(x):
    return jnp.maximum(x, 0.0) + jnp.log1p(jnp.exp(-jnp.abs(x)))


def _const_spec(shape):
    nd = len(shape)
    return pl.BlockSpec(shape, lambda *_: (0,) * nd, pipeline_mode=pl.Buffered(1))


def _conv_silu(xa, hist, cw_ref, cb_ref, cols):
    row8 = lax.broadcasted_iota(jnp.int32, (SUBLANE, xa.shape[1]), 0)
    acc = cb_ref[:, cols] + xa * cw_ref[S_CONV - 1:S_CONV, cols]
    for sh in range(1, S_CONV):
        rolled = pltpu.roll(xa, sh, axis=0)
        top = jnp.where(row8 < sh, pltpu.roll(hist, sh, axis=0), rolled[0:SUBLANE])
        shifted = jnp.concatenate([top, rolled[SUBLANE:]], axis=0)
        acc = acc + shifted * cw_ref[S_CONV - 1 - sh:S_CONV - sh, cols]
    return acc * jax.nn.sigmoid(acc)


_IN_SIZES = (M_QK, M_QK, M_V, M_V, M_HEADS, M_HEADS, S_INNER, S_XBC, S_HEADS, D_MODEL, D_MODEL)
_IN_OFFS = tuple(sum(_IN_SIZES[:i]) for i in range(len(_IN_SIZES) + 1))
IN_WIDTH = _IN_OFFS[-1]
_PACK_SEGMENTS = (
    (_IN_OFFS[0], _IN_OFFS[4], Q0),
    (_IN_OFFS[6], _IN_OFFS[7], Z0),
    (_IN_OFFS[7], _IN_OFFS[8], X0),
    (_IN_OFFS[9], _IN_OFFS[10], GA0),
    (_IN_OFFS[10], _IN_OFFS[11], GB0),
)
PACK_ROWS = 512


def _pack_kernel(wt_ref, o_ref):
    for s0, s1, d0 in _PACK_SEGMENTS:
        for r in range(s0, s1, PACK_ROWS):
            o_ref[:, d0 + r - s0:d0 + r - s0 + PACK_ROWS] = wt_ref[r:r + PACK_ROWS, :].T.astype(BF16)
    kb = wt_ref.shape[1]
    gates = jnp.concatenate(
        [wt_ref[_IN_OFFS[4]:_IN_OFFS[6], :], wt_ref[_IN_OFFS[8]:_IN_OFFS[9], :],
         jnp.zeros((GATE_W - GEND, kb), F32)], axis=0)
    o_ref[:, BIG_W:BIG_W + GATE_W] = gates.T.astype(BF16)


def _pack_w_in(wt):
    kb = LANE
    return pl.pallas_call(
        _pack_kernel,
        grid=(D_MODEL // kb,),
        in_specs=[pl.BlockSpec((IN_WIDTH, kb), lambda i: (0, i))],
        out_specs=pl.BlockSpec((kb, BIG_W + GATE_W), lambda i: (i, 0)),
        out_shape=jax.ShapeDtypeStruct((D_MODEL, BIG_W + GATE_W), BF16),
        compiler_params=pltpu.CompilerParams(
            dimension_semantics=("arbitrary",), vmem_limit_bytes=VMEM_LIMIT),
        name="pack_w_in",
    )(wt)


def _inproj_kernel(x_ref, g_ref, w_ref, *refs, activate):
    if activate:
        xh0_ref, cw_ref, cb_ref, big_ref, gate_ref, u_ref, cs_ref = refs

        @pl.when(pl.program_id(1) == 0)
        def _():
            for s in range(S_XBC // LANE):
                cs_ref[s, 0:SUBLANE, :] = xh0_ref[:, s * LANE:(s + 1) * LANE]
    else:
        big_ref, gate_ref, u_ref = refs
    tm = x_ref.shape[0]
    xf = x_ref[...]
    ms = jnp.mean(xf * xf, axis=-1, keepdims=True)
    u_ref[...] = (xf * lax.rsqrt(ms + EPS) * g_ref[...]).astype(BF16)
    for n in range(BIG_W // NB):
        c0 = n * NB
        acc = jnp.dot(u_ref[...], w_ref[:, c0:c0 + NB], preferred_element_type=F32)
        if c0 < K0:
            acc = acc * (M_DQK ** -0.5)
        if activate:
            if O0 <= c0 < Z0 or c0 >= GA0:
                acc = jax.nn.sigmoid(acc)
            elif Z0 <= c0 < X0:
                acc = acc * jax.nn.sigmoid(acc)
            elif X0 <= c0 < GA0:
                for s in range(NB // LANE):
                    slab = (c0 - X0) // LANE + s
                    cols = slice((c0 - X0) + s * LANE, (c0 - X0) + (s + 1) * LANE)
                    cs_ref[slab, SUBLANE:SUBLANE + tm, :] = acc[:, s * LANE:(s + 1) * LANE]
                    half = 0.5 * cb_ref[:, cols]
                    for tap in range(S_CONV):
                        back = S_CONV - 1 - tap
                        half = half + (cs_ref[slab, pl.ds(SUBLANE - back, tm, stride=1), :]
                                       * (0.5 * cw_ref[tap:tap + 1, cols]))
                    cs_ref[slab, 0:SUBLANE, :] = cs_ref[slab, tm:tm + SUBLANE, :]
                    big_ref[:, c0 + s * LANE:c0 + (s + 1) * LANE] = (half + half * jnp.tanh(half)).astype(BF16)
                continue
        big_ref[:, c0:c0 + NB] = acc.astype(BF16)
    gate_ref[...] = jnp.dot(u_ref[...], w_ref[:, BIG_W:BIG_W + GATE_W], preferred_element_type=F32)


def _inproj(rows, g, w_all, tm, conv=None, batch=1):
    n = rows.shape[0]
    tiles = n // batch // tm
    row_map = lambda b, j: (b * tiles + j, 0)
    activate = conv is not None
    extra = list(conv) if activate else []
    scratch = [pltpu.VMEM((tm, D_MODEL), BF16)]
    if activate:
        scratch.append(pltpu.VMEM((S_XBC // LANE, SUBLANE + tm, LANE), F32))
    return pl.pallas_call(
        functools.partial(_inproj_kernel, activate=activate),
        grid=(batch, tiles),
        in_specs=[
            pl.BlockSpec((tm, D_MODEL), row_map),
            _const_spec((1, D_MODEL)),
            _const_spec((D_MODEL, BIG_W + GATE_W)),
        ] + [_const_spec(a.shape) for a in extra],
        out_specs=[
            pl.BlockSpec((tm, BIG_W), row_map),
            pl.BlockSpec((tm, GATE_W), row_map),
        ],
        out_shape=[
            jax.ShapeDtypeStruct((n, BIG_W), BF16),
            jax.ShapeDtypeStruct((n, GATE_W), F32),
        ],
        scratch_shapes=scratch,
        compiler_params=pltpu.CompilerParams(
            dimension_semantics=("arbitrary", "arbitrary"), vmem_limit_bytes=VMEM_LIMIT),
        name="inproj_act" if activate else "inproj_raw",
    )(rows, g, w_all, *extra)


def _gate_block(pre, alog_row, masked, tril3):
    L = pre.shape[0]
    lane = lax.broadcasted_iota(jnp.int32, (L, GATE_W), 1)
    sc = GATE_CAP * jnp.tanh(pre / GATE_CAP)
    i_log = sc * LOG2E
    f_log = -_softplus(-sc) * LOG2E
    dt = _softplus(pre)
    if masked:
        valid = lax.broadcasted_iota(jnp.int32, (L, GATE_W), 0) >= (L - N_META)
        i_log = jnp.where(valid, i_log, -jnp.inf)
        f_log = jnp.where(valid, f_log, 0.0)
        dt = jnp.where(valid, dt, 0.0)
    act = jnp.where(lane < GF0, i_log, jnp.where(lane < GDT0, f_log, jnp.where(lane < GEND, dt, 0.0)))
    a_row = -jnp.exp(alog_row) * LOG2E
    cs = jnp.where(lane < GF0, 0.0, jnp.where(lane < GDT0, act, jnp.where(lane < GEND, act * a_row, 0.0)))
    hi = cs.astype(BF16)
    rem = cs - hi.astype(F32)
    mid = rem.astype(BF16)
    lo = (rem - mid.astype(F32)).astype(BF16)
    cum = jnp.dot(tril3, jnp.concatenate([hi, mid, lo], axis=0), preferred_element_type=F32)
    return act, cum


def _tril3(L):
    assert L & (L - 1) == 0
    row_i = lax.broadcasted_iota(jnp.int32, (L, 3 * L), 0)
    col_i = lax.broadcasted_iota(jnp.int32, (L, 3 * L), 1)
    return (jnp.bitwise_and(col_i, L - 1) <= row_i).astype(F32).astype(BF16)


def _lane_rep(x, col):
    return jnp.broadcast_to(x[:, col:col + 1], (x.shape[0], LANE))


def _split3(x):
    hi = x.astype(BF16).astype(F32)
    rem = x - hi
    mid = rem.astype(BF16).astype(F32)
    lo = (rem - mid).astype(BF16).astype(F32)
    return hi, mid, lo


def _outer_sums(x_rows, y_rows):
    n, L = x_rows.shape
    xs, ys = _split3(x_rows), _split3(y_rows)
    pad = 16 - 9
    row = lax.broadcasted_iota(jnp.int32, (6, 2 * L), 0)
    lane = lax.broadcasted_iota(jnp.int32, (6, 2 * L), 1)
    select = (jnp.bitwise_and(row, 1) == (lane >= L).astype(jnp.int32)).astype(F32)
    out = []
    for p in range(n // 2):
        sl = slice(2 * p, 2 * p + 2)
        a = jnp.concatenate([xs[0][sl], xs[1][sl], xs[2][sl], jnp.ones((3, L), F32),
                             jnp.zeros((pad, L), F32)], axis=0)
        b_var = jnp.concatenate(
            [jnp.concatenate([t[2 * p:2 * p + 1], t[2 * p + 1:2 * p + 2]], axis=1) for t in ys], axis=0)
        b = jnp.concatenate([select, b_var, jnp.zeros((pad, 2 * L), F32)], axis=0)
        d = lax.dot_general(a.astype(BF16), b.astype(BF16), (((0,), (0,)), ((), ())),
                            preferred_element_type=F32)
        out += [d[:, :L], d[:, L:]]
    return out


def _mlstm_state(k, v, bt_b, it_b, b_tot, m_prev, ct_ref, n_ref, m_ref, h):
    w_end = b_tot - bt_b + it_b
    m_loc = jnp.max(w_end, axis=0, keepdims=True)
    wgt = jnp.exp2(w_end - m_loc)
    kw = k.astype(F32) * wgt
    s_loc = lax.dot_general(kw.astype(BF16), v, (((0,), (0,)), ((), ())), preferred_element_type=F32)
    n_loc = jnp.sum(kw, axis=0, keepdims=True)
    m_new = jnp.maximum(b_tot + m_prev, m_loc)
    a_dec = jnp.exp2(b_tot + m_prev - m_new)
    s_dec = jnp.exp2(m_loc - m_new)
    ct_ref[h] = (jnp.concatenate([a_dec, a_dec], axis=1) * ct_ref[h]
                 + jnp.concatenate([s_dec, s_dec], axis=1) * s_loc)
    n_ref[h:h + 1, :] = a_dec * n_ref[h:h + 1, :] + s_dec * n_loc
    m_ref[h:h + 1, :] = m_new


def _head_expand(act, cum, expand_ref):
    L = act.shape[0]
    lane = lax.broadcasted_iota(jnp.int32, (L, GATE_W), 1)
    is_dt = (lane >= GDT0) & (lane < GEND)
    tot = cum[L - 1:L, :]
    e_cum = jnp.where(is_dt, jnp.exp2(cum), 0.0)
    e_end = jnp.where(is_dt, jnp.exp2(tot - cum) * act, 0.0)

    def split(x):
        hi = x.astype(BF16)
        lo = (x - hi.astype(F32)).astype(BF16)
        return jnp.concatenate([hi, lo], axis=1)

    lhs = jnp.concatenate([split(e_cum), split(e_end)], axis=0)
    return jnp.dot(lhs, expand_ref[...], preferred_element_type=F32)


def _chunk_step(r0, big_ref, gs, p, ct_ref, n_ref, m_ref, st_ref, hm_ref, ys_ref, hmask):
    L = CHUNK
    rows = pl.ds(r0, L)
    row_i = lax.broadcasted_iota(jnp.int32, (L, L), 0)
    col_i = lax.broadcasted_iota(jnp.int32, (L, L), 1)
    causal = col_i <= row_i
    neg_inf = -jnp.inf

    rep_ref = gs["rep"]
    act_t = gs["act_t"][rows, :]
    cum_t = gs["cum_t"][rows, :]
    e_rows = pl.multiple_of(2 * r0, 2 * L)

    zero_blk = jnp.zeros((L, LANE), BF16)

    def paired_nt(lhs_pair, rhs0, rhs1):
        rhs = jnp.concatenate([jnp.concatenate([rhs0, zero_blk], axis=1),
                               jnp.concatenate([zero_blk, rhs1], axis=1)], axis=0)
        out = lax.dot_general(lhs_pair, rhs, (((1,), (1,)), ((), ())), preferred_element_type=F32)
        return out[:, :LANE], out[:, LANE:]

    qk_heads = []
    for pr in range(M_HEADS // 2):
        qk_heads += paired_nt(
            big_ref[rows, Q0 + 2 * pr * M_DQK:Q0 + 2 * (pr + 1) * M_DQK],
            big_ref[rows, K0 + 2 * pr * M_DQK:K0 + (2 * pr + 1) * M_DQK],
            big_ref[rows, K0 + (2 * pr + 1) * M_DQK:K0 + 2 * (pr + 1) * M_DQK])

    dmat_ref = gs["dmat"]

    hs = range(M_HEADS)
    q = [big_ref[rows, Q0 + h * M_DQK:Q0 + (h + 1) * M_DQK] for h in hs]
    k = [big_ref[rows, K0 + h * M_DQK:K0 + (h + 1) * M_DQK] for h in hs]
    v = [big_ref[rows, V0 + h * M_DV:V0 + (h + 1) * M_DV] for h in hs]
    bt_b = [rep_ref[h, rows, :] for h in hs]
    it_b = [rep_ref[M_HEADS + h, rows, :] for h in hs]
    b_tot = [x[L - 1:L, :] for x in bt_b]
    m_prev = [m_ref[h:h + 1, :] for h in hs]
    d_log = [jnp.where(causal, dmat_ref[h, rows, :], neg_inf) for h in hs]
    mx = [jnp.max(d, axis=1, keepdims=True) for d in d_log]
    inter_log = [bt_b[h] + m_prev[h] for h in hs]
    m_t = [jnp.maximum(inter_log[h], mx[h]) for h in hs]
    w_ts = [jnp.exp2(d_log[h] - m_t[h]) * qk_heads[h] for h in hs]
    inter = [jnp.exp2(inter_log[h] - m_t[h]) for h in hs]
    q_f = [x.astype(F32) for x in q]
    num = [jnp.dot(jnp.concatenate([w_ts[h].astype(BF16), (q_f[h] * inter[h]).astype(BF16)], axis=1),
                   jnp.concatenate([v[h], ct_ref[h].astype(BF16)], axis=0), preferred_element_type=F32)
           for h in hs]
    den = [jnp.sum(w_ts[h], axis=1, keepdims=True)
           + inter[h] * jnp.sum(q_f[h] * n_ref[h:h + 1, :], axis=1, keepdims=True) for h in hs]
    for h in hs:
        denom = jnp.maximum(jnp.abs(den[h]), jnp.exp2(-m_t[h]))
        ms = jnp.mean(num[h] * num[h], axis=1, keepdims=True)
        rs = lax.rsqrt(ms + EPS * (denom * denom))
        y = num[h] * jnp.concatenate([rs, rs], axis=1) * p["mng"][:, h * M_DV:(h + 1) * M_DV]
        o_gate = big_ref[rows, O0 + h * M_DV:O0 + (h + 1) * M_DV]
        hm_ref[rows, h * M_DV:(h + 1) * M_DV] = y.astype(BF16) * o_gate
    for h in hs:
        _mlstm_state(k[h], v[h], bt_b[h], it_b[h], b_tot[h], m_prev[h], ct_ref, n_ref, m_ref, h)

    e_ref = gs["expanded"]
    cb_groups = []
    for pr in range(S_GROUPS // 2):
        cb_groups += paired_nt(
            big_ref[rows, X0 + XC0 + 2 * pr * S_STATE:X0 + XC0 + 2 * (pr + 1) * S_STATE],
            big_ref[rows, X0 + XB0 + 2 * pr * S_STATE:X0 + XB0 + (2 * pr + 1) * S_STATE],
            big_ref[rows, X0 + XB0 + (2 * pr + 1) * S_STATE:X0 + XB0 + 2 * (pr + 1) * S_STATE])
    for g in range(S_GROUPS):
        gcols = slice(g * S_GW, (g + 1) * S_GW)
        bg = big_ref[rows, X0 + XB0 + g * S_STATE:X0 + XB0 + (g + 1) * S_STATE]
        cg = big_ref[rows, X0 + XC0 + g * S_STATE:X0 + XC0 + (g + 1) * S_STATE]
        xg_b = big_ref[rows, X0 + g * S_GW:X0 + (g + 1) * S_GW]
        xg = xg_b.astype(F32)
        cbm = cb_groups[g]
        w_heads = []
        for j in range(S_HPG):
            gk = GDT0 + g * S_HPG + j
            dt_r = act_t[gk:gk + 1, :]
            dec = jnp.exp2(jnp.where(causal, dmat_ref[M_HEADS + g * S_HPG + j, rows, :], neg_inf))
            w_heads.append((cbm * dec * dt_r).astype(BF16))
        y_diag = jnp.zeros((L, S_GW), F32)
        for j in range(0, S_HPG, 2):
            y_diag = y_diag + jnp.dot(
                jnp.concatenate([w_heads[j], w_heads[j + 1]], axis=1),
                jnp.concatenate([xg_b * hmask[j], xg_b * hmask[j + 1]], axis=0),
                preferred_element_type=F32)
        e_cum = e_ref[pl.ds(e_rows, L), gcols]
        e_end = e_ref[pl.ds(e_rows + L, L), gcols]
        e_tot = e_ref[pl.ds(e_rows + L - SUBLANE, SUBLANE), gcols][SUBLANE - 1:SUBLANE, :]
        y_off = jnp.dot(cg, st_ref[g].astype(BF16), preferred_element_type=F32) * e_cum
        y = y_diag + y_off + p["sd"][:, gcols] * xg
        yz = y * big_ref[rows, Z0 + g * S_GW:Z0 + (g + 1) * S_GW].astype(F32)
        ms = jnp.mean(yz * yz, axis=1, keepdims=True)
        ys_ref[rows, gcols] = (yz * lax.rsqrt(ms + EPS) * p["sng"][:, gcols]).astype(BF16)
        xw = (xg * e_end).astype(BF16)
        s_new = lax.dot_general(bg, xw, (((0,), (0,)), ((), ())), preferred_element_type=F32)
        st_ref[g] = st_ref[g] * e_tot + s_new


_SMALL_KEYS = ("gbias", "alog", "mng", "sd", "sng", "expand")


def _init_kernel(big_ref, gate_ref, gbias_ref, alog_ref, cw_ref, cb_ref, expand_ref,
                 ct_ref, n_ref, m_ref, st_ref, xh_ref):
    L = CHUNK
    act, cum = _gate_block(gate_ref[...] + gbias_ref[...], alog_ref[...], True, _tril3(L))
    ct_ref[...] = jnp.zeros_like(ct_ref)
    n_ref[...] = jnp.zeros_like(n_ref)
    m_ref[...] = jnp.zeros_like(m_ref)
    for h in range(M_HEADS):
        k = big_ref[:, K0 + h * M_DQK:K0 + (h + 1) * M_DQK]
        v = big_ref[:, V0 + h * M_DV:V0 + (h + 1) * M_DV]
        gi, gf = GI0 + h, GF0 + h
        bt_b = _lane_rep(cum, gf)
        _mlstm_state(k, v, bt_b, _lane_rep(act, gi), bt_b[L - 1:L, :], m_ref[h:h + 1, :],
                     ct_ref, n_ref, m_ref, h)

    e_all = _head_expand(act, cum, expand_ref)
    valid = lax.broadcasted_iota(jnp.int32, (L, S_XBC), 0) >= (L - N_META)
    xa = jnp.where(valid, big_ref[:, X0:X0 + S_XBC].astype(F32), 0.0)
    xh_ref[...] = xa[L - SUBLANE:L, :]
    xc = _conv_silu(xa, jnp.zeros((SUBLANE, S_XBC), F32), cw_ref, cb_ref, slice(0, S_XBC))
    for g in range(S_GROUPS):
        gcols = slice(g * S_GW, (g + 1) * S_GW)
        bg = xc[:, XB0 + g * S_STATE:XB0 + (g + 1) * S_STATE].astype(BF16)
        xw = (xc[:, gcols] * e_all[L:2 * L, gcols]).astype(BF16)
        st_ref[g] = lax.dot_general(bg, xw, (((0,), (0,)), ((), ())), preferred_element_type=F32)


_STATE_SHAPES = (
    (M_HEADS, M_DQK, M_DV),
    (SUBLANE, M_DQK),
    (SUBLANE, LANE),
    (S_GROUPS, S_STATE, S_GW),
    (SUBLANE, S_XBC),
)


def _init_state(big_m, gate_m, consts):
    return pl.pallas_call(
        _init_kernel,
        grid=(1,),
        in_specs=[_const_spec(big_m.shape), _const_spec(gate_m.shape)] + [_const_spec(a.shape) for a in consts],
        out_specs=[_const_spec(s) for s in _STATE_SHAPES],
        out_shape=[jax.ShapeDtypeStruct(s, F32) for s in _STATE_SHAPES],
        compiler_params=pltpu.CompilerParams(
            dimension_semantics=("arbitrary",), vmem_limit_bytes=VMEM_LIMIT),
        name="init_state",
    )(big_m, gate_m, *consts)


def _mixer_kernel(big_ref, gate_ref, x_ref, *refs):
    ns = len(_SMALL_KEYS)
    p = dict(zip(_SMALL_KEYS, refs[:ns]))
    ct0, n0, m0, st0 = refs[ns:ns + 4]
    mproj_ref, sproj_ref, wout_ref = refs[ns + 4:ns + 7]
    h1_ref = refs[ns + 7]
    ct_ref, n_ref, m_ref, st_ref, hm_ref, ys_ref, mg_ref = refs[ns + 8:ns + 15]
    gs = dict(zip(("rep", "act_t", "cum_t", "expanded", "dmat"), refs[ns + 15:]))

    @pl.when(pl.program_id(1) == 0)
    def _():
        ct_ref[...] = ct0[...]
        n_ref[...] = n0[...]
        m_ref[...] = m0[...]
        st_ref[...] = st0[...]

    def prologue():
        tril3 = _tril3(CHUNK)
        slices = [slice(c * CHUNK, (c + 1) * CHUNK) for c in range(CPT)]
        gates = [_gate_block(gate_ref[rs, :] + p["gbias"][...], p["alog"][...], False, tril3) for rs in slices]
        trans = [(act.T, cum.T) for act, cum in gates]
        for rs, (act_t, cum_t) in zip(slices, trans):
            gs["act_t"][rs, :] = act_t
            gs["cum_t"][rs, :] = cum_t
        for c, (rs, (act, cum)) in enumerate(zip(slices, gates)):
            gs["expanded"][2 * c * CHUNK:2 * (c + 1) * CHUNK, :] = _head_expand(act, cum, p["expand"])
        for rs, (act_t, cum_t) in zip(slices, trans):
            bt_rows = cum_t[GF0:GDT0, :]
            ca_rows = cum_t[GDT0:GEND, :]
            mats = (_outer_sums(bt_rows, act_t[GI0:GF0, :] - bt_rows)
                    + _outer_sums(ca_rows, -ca_rows))
            for i, mat in enumerate(mats):
                gs["dmat"][i, rs, :] = mat
        for rs, (act, cum) in zip(slices, gates):
            for h in range(M_HEADS):
                gs["rep"][h, rs, :] = _lane_rep(cum, GF0 + h)
                gs["rep"][M_HEADS + h, rs, :] = _lane_rep(act, GI0 + h)

    prologue()

    lane_g = lax.broadcasted_iota(jnp.int32, (CHUNK, S_GW), 1)
    hmask = [((lane_g >= j * S_HEADDIM) & (lane_g < (j + 1) * S_HEADDIM)).astype(F32).astype(BF16)
             for j in range(S_HPG)]

    def body(c, carry):
        r0 = pl.multiple_of(c * CHUNK, CHUNK)
        _chunk_step(r0, big_ref, gs, p, ct_ref, n_ref, m_ref, st_ref, hm_ref, ys_ref, hmask)
        return carry

    for c in range(CPT):
        body(c, 0)

    for nb in range(D_MODEL // FB):
        cols = slice(nb * FB, (nb + 1) * FB)
        br_a = jnp.dot(hm_ref[...], mproj_ref[:, cols], preferred_element_type=F32)
        br_b = jnp.dot(ys_ref[...], sproj_ref[:, cols], preferred_element_type=F32)
        ga = big_ref[:, GA0 + nb * FB:GA0 + (nb + 1) * FB].astype(F32)
        gb = big_ref[:, GB0 + nb * FB:GB0 + (nb + 1) * FB].astype(F32)
        mg_ref[:, cols] = (ga * br_a + gb * br_b).astype(BF16)
    for nb in range(D_MODEL // FB):
        cols = slice(nb * FB, (nb + 1) * FB)
        h1_ref[:, cols] = x_ref[:, cols] + jnp.dot(mg_ref[...], wout_ref[:, cols], preferred_element_type=F32)


def _mixer(big, gate, x2, small, init, mproj, sproj, wout, batch):
    n = x2.shape[0]
    tm = MIX_TM
    tiles = n // batch // tm
    row_map = lambda b, j: (b * tiles + j, 0)
    state_shapes = _STATE_SHAPES[:4]
    return pl.pallas_call(
        _mixer_kernel,
        grid=(batch, tiles),
        in_specs=[
            pl.BlockSpec((tm, BIG_W), row_map),
            pl.BlockSpec((tm, GATE_W), row_map),
            pl.BlockSpec((tm, D_MODEL), row_map),
        ] + [_const_spec(s.shape) for s in small]
          + [_const_spec(s) for s in state_shapes]
          + [_const_spec((D_MODEL, D_MODEL))] * 3,
        out_specs=pl.BlockSpec((tm, D_MODEL), row_map),
        out_shape=jax.ShapeDtypeStruct((n, D_MODEL), F32),
        scratch_shapes=[pltpu.VMEM(s, F32) for s in state_shapes] + [
            pltpu.VMEM((tm, M_V), BF16),
            pltpu.VMEM((tm, S_INNER), BF16),
            pltpu.VMEM((tm, D_MODEL), BF16),
            pltpu.VMEM((2 * M_HEADS, tm, LANE), F32),
            pltpu.VMEM((tm, GATE_W), F32),
            pltpu.VMEM((tm, GATE_W), F32),
            pltpu.VMEM((2 * tm, S_INNER), F32),
            pltpu.VMEM((M_HEADS + S_HEADS, tm, CHUNK), F32),
        ],
        compiler_params=pltpu.CompilerParams(
            dimension_semantics=("arbitrary", "arbitrary"), vmem_limit_bytes=VMEM_LIMIT),
        name="mixer",
    )(big, gate, x2, *small, *init[:4], mproj, sproj, wout)


def _ffn_kernel(h_ref, g2_ref, gf_ref, w1_ref, w2_ref, o_ref, u_ref, hid_ref):
    h = h_ref[...]
    ms = jnp.mean(h * h, axis=-1, keepdims=True)
    u_ref[...] = (h * lax.rsqrt(ms + EPS) * g2_ref[...]).astype(BF16)
    for jb in range(D_FF // FB):
        gate = jnp.dot(u_ref[...], w1_ref[:, jb * FB:(jb + 1) * FB], preferred_element_type=F32)
        up = jnp.dot(u_ref[...], w1_ref[:, D_FF + jb * FB:D_FF + (jb + 1) * FB], preferred_element_type=F32)
        hid_ref[:, jb * FB:(jb + 1) * FB] = (gate * jax.nn.sigmoid(gate) * up).astype(BF16)
    for nb in range(D_MODEL // FB):
        cols = slice(nb * FB, (nb + 1) * FB)
        o_ref[:, cols] = h_ref[:, cols] + jnp.dot(hid_ref[...], w2_ref[:, cols], preferred_element_type=F32)
    h2 = o_ref[...]
    ms2 = jnp.mean(h2 * h2, axis=-1, keepdims=True)
    o_ref[...] = h2 * lax.rsqrt(ms2 + EPS) * gf_ref[...]


def _ffn(h1, g2, gf, w1, w2):
    n = h1.shape[0]
    tm = FFN_TM
    return pl.pallas_call(
        _ffn_kernel,
        grid=(n // tm,),
        in_specs=[
            pl.BlockSpec((tm, D_MODEL), lambda i: (i, 0)),
            _const_spec((1, D_MODEL)),
            _const_spec((1, D_MODEL)),
            _const_spec((D_MODEL, 2 * D_FF)),
            _const_spec((D_FF, D_MODEL)),
        ],
        out_specs=pl.BlockSpec((tm, D_MODEL), lambda i: (i, 0)),
        out_shape=jax.ShapeDtypeStruct((n, D_MODEL), F32),
        scratch_shapes=[pltpu.VMEM((tm, D_MODEL), BF16), pltpu.VMEM((tm, D_FF), BF16)],
        compiler_params=pltpu.CompilerParams(
            dimension_semantics=("arbitrary",), vmem_limit_bytes=VMEM_LIMIT),
        name="ffn",
    )(h1, g2, gf, w1, w2)


def _pad_lanes(row, width):
    return jnp.pad(row, ((0, 0), (0, width - row.shape[1])))


def kernel(x, meta, norm1_g, w_in, m_igate_b, m_fgate_b, m_norm_g, m_proj, s_conv_w, s_conv_b,
           s_dt_bias, s_A_log, s_D, s_norm_g, s_proj, w_out, norm2_g, w_ffn_in, w_ffn_out, norm_f_g):
    bsz, seq, dm = x.shape
    assert dm == D_MODEL and seq % TM == 0 and w_in.shape[0] == 1
    w_all = _pack_w_in(w_in[0].T)

    gbias = _pad_lanes(jnp.concatenate([m_igate_b[0], m_fgate_b[0], s_dt_bias[0]])[None].astype(F32), GATE_W)
    alog = _pad_lanes(jnp.concatenate([jnp.zeros((GDT0,), F32), s_A_log[0].astype(F32)])[None], GATE_W)
    conv_w = jnp.pad(s_conv_w[0].astype(F32), ((0, SUBLANE - S_CONV), (0, 0)))
    conv_b = s_conv_b[0][None].astype(F32)
    head_of_lane = jnp.arange(S_INNER) // S_HEADDIM
    expand1 = (jnp.arange(GATE_W)[:, None] == GDT0 + head_of_lane[None, :]).astype(BF16)
    expand = jnp.concatenate([expand1, expand1], axis=0)
    small = (
        gbias, alog,
        m_norm_g[0].reshape(1, M_V).astype(F32),
        jnp.repeat(s_D[0].astype(F32), S_HEADDIM)[None],
        s_norm_g[0].reshape(1, S_INNER).astype(F32),
        expand,
    )
    g1 = norm1_g[0][None].astype(F32)

    meta_rows = jnp.concatenate([jnp.zeros((CHUNK - N_META, D_MODEL), x.dtype), meta.astype(x.dtype)], axis=0)
    big_m, gate_m = _inproj(meta_rows, g1, w_all, CHUNK)
    init = _init_state(big_m, gate_m, (gbias, alog, conv_w, conv_b, expand))

    x2 = x.reshape(bsz * seq, dm)
    big, gate = _inproj(x2, g1, w_all, TM, conv=(init[4], conv_w, conv_b), batch=bsz)
    h1 = _mixer(big, gate, x2, small, init, m_proj[0].astype(BF16), s_proj[0].astype(BF16),
                w_out[0].astype(BF16), bsz)
    out = _ffn(h1, norm2_g[0][None].astype(F32), norm_f_g[None].astype(F32),
               w_ffn_in[0].astype(BF16), w_ffn_out[0].astype(BF16))
    return out.reshape(bsz, seq, dm)
```

```python
import functools
import math

import jax
import jax.numpy as jnp
from jax import lax
from jax.experimental import pallas as pl
from jax.experimental.pallas import tpu as pltpu

F32 = jnp.float32
BF16 = jnp.bfloat16
HIGHEST = lax.Precision.HIGHEST
LOG2E = math.log2(math.e)

D_MODEL = 1024
N_META = 16
CHUNK = 128
EPS = 1e-6

M_HEADS = 4
M_DQK = 128
M_DV = 256
M_QK = M_HEADS * M_DQK
M_V = M_HEADS * M_DV
GATE_CAP = 15.0

S_HEADDIM = 64
S_INNER = D_MODEL
S_HEADS = S_INNER // S_HEADDIM
S_GROUPS = 4
S_HPG = S_HEADS // S_GROUPS
S_STATE = 128
S_CONV = 4
S_GW = S_HPG * S_HEADDIM
S_XBC = S_INNER + 2 * S_GROUPS * S_STATE
XB0 = S_INNER
XC0 = S_INNER + S_GROUPS * S_STATE

D_FF = 2816

Q0 = 0
K0 = Q0 + M_QK
V0 = K0 + M_QK
O0 = V0 + M_V
Z0 = O0 + M_V
X0 = Z0 + S_INNER
GA0 = X0 + S_XBC
GB0 = GA0 + D_MODEL
BIG_W = GB0 + D_MODEL
GATE_W = 128
GI0, GF0, GDT0, GEND = 0, M_HEADS, 2 * M_HEADS, 2 * M_HEADS + S_HEADS

LANE = 128
SUBLANE = 8
TM = 512
MIX_TM = 512
FFN_TM = 1024
CPT = MIX_TM // CHUNK
PROJ_CHUNKS = 2
NB = 512
FB = 256
VMEM_LIMIT = 56 * 1024 * 1024


def _softplus(x):
    return jnp.maximum(x, 0.0) + jnp.log1p(jnp.exp(-jnp.abs(x)))


def _const_spec(shape):
    nd = len(shape)
    return pl.BlockSpec(shape, lambda *_: (0,) * nd, pipeline_mode=pl.Buffered(1))


def _conv_silu(xa, hist, cw_ref, cb_ref, cols):
    row8 = lax.broadcasted_iota(jnp.int32, (SUBLANE, xa.shape[1]), 0)
    acc = cb_ref[:, cols] + xa * cw_ref[S_CONV - 1:S_CONV, cols]
    for sh in range(1, S_CONV):
        rolled = pltpu.roll(xa, sh, axis=0)
        top = jnp.where(row8 < sh, pltpu.roll(hist, sh, axis=0), rolled[0:SUBLANE])
        shifted = jnp.concatenate([top, rolled[SUBLANE:]], axis=0)
        acc = acc + shifted * cw_ref[S_CONV - 1 - sh:S_CONV - sh, cols]
    return acc * jax.nn.sigmoid(acc)


_IN_SIZES = (M_QK, M_QK, M_V, M_V, M_HEADS, M_HEADS, S_INNER, S_XBC, S_HEADS, D_MODEL, D_MODEL)
_IN_OFFS = tuple(sum(_IN_SIZES[:i]) for i in range(len(_IN_SIZES) + 1))
IN_WIDTH = _IN_OFFS[-1]
_PACK_SEGMENTS = (
    (_IN_OFFS[0], _IN_OFFS[4], Q0),
    (_IN_OFFS[6], _IN_OFFS[7], Z0),
    (_IN_OFFS[7], _IN_OFFS[8], X0),
    (_IN_OFFS[9], _IN_OFFS[10], GA0),
    (_IN_OFFS[10], _IN_OFFS[11], GB0),
)
PACK_ROWS = 512


def _pack_kernel(wt_ref, o_ref):
    for s0, s1, d0 in _PACK_SEGMENTS:
        for r in range(s0, s1, PACK_ROWS):
            o_ref[:, d0 + r - s0:d0 + r - s0 + PACK_ROWS] = wt_ref[r:r + PACK_ROWS, :].T.astype(BF16)
    kb = wt_ref.shape[1]
    gates = jnp.concatenate(
        [wt_ref[_IN_OFFS[4]:_IN_OFFS[6], :], wt_ref[_IN_OFFS[8]:_IN_OFFS[9], :],
         jnp.zeros((GATE_W - GEND, kb), F32)], axis=0)
    o_ref[:, BIG_W:BIG_W + GATE_W] = gates.T.astype(BF16)


def _pack_w_in(wt):
    kb = LANE
    return pl.pallas_call(
        _pack_kernel,
        grid=(D_MODEL // kb,),
        in_specs=[pl.BlockSpec((IN_WIDTH, kb), lambda i: (0, i))],
        out_specs=pl.BlockSpec((kb, BIG_W + GATE_W), lambda i: (i, 0)),
        out_shape=jax.ShapeDtypeStruct((D_MODEL, BIG_W + GATE_W), BF16),
        compiler_params=pltpu.CompilerParams(
            dimension_semantics=("arbitrary",), vmem_limit_bytes=VMEM_LIMIT),
        name="pack_w_in",
    )(wt)


def _inproj_kernel(x_ref, g_ref, w_ref, *refs, activate):
    if activate:
        xh0_ref, cw_ref, cb_ref, big_ref, gate_ref, u_ref, cs_ref = refs

        @pl.when(pl.program_id(1) == 0)
        def _():
            for s in range(S_XBC // LANE):
                cs_ref[s, 0:SUBLANE, :] = xh0_ref[:, s * LANE:(s + 1) * LANE]
    else:
        big_ref, gate_ref, u_ref = refs
    tm = x_ref.shape[0]
    xf = x_ref[...]
    ms = jnp.mean(xf * xf, axis=-1, keepdims=True)
    u_ref[...] = (xf * lax.rsqrt(ms + EPS) * g_ref[...]).astype(BF16)
    for n in range(BIG_W // NB):
        c0 = n * NB
        acc = jnp.dot(u_ref[...], w_ref[:, c0:c0 + NB], preferred_element_type=F32)
        if c0 < K0:
            acc = acc * (M_DQK ** -0.5)
        if activate:
            if O0 <= c0 < Z0 or c0 >= GA0:
                acc = jax.nn.sigmoid(acc)
            elif Z0 <= c0 < X0:
                acc = acc * jax.nn.sigmoid(acc)
            elif X0 <= c0 < GA0:
                for s in range(NB // LANE):
                    slab = (c0 - X0) // LANE + s
                    cols = slice((c0 - X0) + s * LANE, (c0 - X0) + (s + 1) * LANE)
                    cs_ref[slab, SUBLANE:SUBLANE + tm, :] = acc[:, s * LANE:(s + 1) * LANE]
                    half = 0.5 * cb_ref[:, cols]
                    for tap in range(S_CONV):
                        back = S_CONV - 1 - tap
                        half = half + (cs_ref[slab, pl.ds(SUBLANE - back, tm, stride=1), :]
                                       * (0.5 * cw_ref[tap:tap + 1, cols]))
                    cs_ref[slab, 0:SUBLANE, :] = cs_ref[slab, tm:tm + SUBLANE, :]
                    big_ref[:, c0 + s * LANE:c0 + (s + 1) * LANE] = (half + half * jnp.tanh(half)).astype(BF16)
                continue
        big_ref[:, c0:c0 + NB] = acc.astype(BF16)
    gate_ref[...] = jnp.dot(u_ref[...], w_ref[:, BIG_W:BIG_W + GATE_W], preferred_element_type=F32)


def _inproj(rows, g, w_all, tm, conv=None, batch=1):
    n = rows.shape[0]
    tiles = n // batch // tm
    row_map = lambda b, j: (b * tiles + j, 0)
    activate = conv is not None
    extra = list(conv) if activate else []
    scratch = [pltpu.VMEM((tm, D_MODEL), BF16)]
    if activate:
        scratch.append(pltpu.VMEM((S_XBC // LANE, SUBLANE + tm, LANE), F32))
    return pl.pallas_call(
        functools.partial(_inproj_kernel, activate=activate),
        grid=(batch, tiles),
        in_specs=[
            pl.BlockSpec((tm, D_MODEL), row_map),
            _const_spec((1, D_MODEL)),
            _const_spec((D_MODEL, BIG_W + GATE_W)),
        ] + [_const_spec(a.shape) for a in extra],
        out_specs=[
            pl.BlockSpec((tm, BIG_W), row_map),
            pl.BlockSpec((tm, GATE_W), row_map),
        ],
        out_shape=[
            jax.ShapeDtypeStruct((n, BIG_W), BF16),
            jax.ShapeDtypeStruct((n, GATE_W), F32),
        ],
        scratch_shapes=scratch,
        compiler_params=pltpu.CompilerParams(
            dimension_semantics=("arbitrary", "arbitrary"), vmem_limit_bytes=VMEM_LIMIT),
        name="inproj_act" if activate else "inproj_raw",
    )(rows, g, w_all, *extra)


def _gate_block(pre, alog_row, masked, tril3):
    L = pre.shape[0]
    lane = lax.broadcasted_iota(jnp.int32, (L, GATE_W), 1)
    sc = GATE_CAP * jnp.tanh(pre / GATE_CAP)
    i_log = sc * LOG2E
    f_log = -_softplus(-sc) * LOG2E
    dt = _softplus(pre)
    if masked:
        valid = lax.broadcasted_iota(jnp.int32, (L, GATE_W), 0) >= (L - N_META)
        i_log = jnp.where(valid, i_log, -jnp.inf)
        f_log = jnp.where(valid, f_log, 0.0)
        dt = jnp.where(valid, dt, 0.0)
    act = jnp.where(lane < GF0, i_log, jnp.where(lane < GDT0, f_log, jnp.where(lane < GEND, dt, 0.0)))
    a_row = -jnp.exp(alog_row) * LOG2E
    cs = jnp.where(lane < GF0, 0.0, jnp.where(lane < GDT0, act, jnp.where(lane < GEND, act * a_row, 0.0)))
    hi = cs.astype(BF16)
    rem = cs - hi.astype(F32)
    mid = rem.astype(BF16)
    lo = (rem - mid.astype(F32)).astype(BF16)
    cum = jnp.dot(tril3, jnp.concatenate([hi, mid, lo], axis=0), preferred_element_type=F32)
    return act, cum


def _tril3(L):
    assert L & (L - 1) == 0
    row_i = lax.broadcasted_iota(jnp.int32, (L, 3 * L), 0)
    col_i = lax.broadcasted_iota(jnp.int32, (L, 3 * L), 1)
    return (jnp.bitwise_and(col_i, L - 1) <= row_i).astype(F32).astype(BF16)


def _lane_rep(x, col):
    return jnp.broadcast_to(x[:, col:col + 1], (x.shape[0], LANE))


def _mlstm_state(k, v, bt_b, it_b, b_tot, m_prev, ct_ref, n_ref, m_ref, h):
    w_end = b_tot - bt_b + it_b
    m_loc = jnp.max(w_end, axis=0, keepdims=True)
    wgt = jnp.exp2(w_end - m_loc)
    kw = k.astype(F32) * wgt
    s_loc = lax.dot_general(kw.astype(BF16), v, (((0,), (0,)), ((), ())), preferred_element_type=F32)
    n_loc = jnp.sum(kw, axis=0, keepdims=True)
    m_new = jnp.maximum(b_tot + m_prev, m_loc)
    a_dec = jnp.exp2(b_tot + m_prev - m_new)
    s_dec = jnp.exp2(m_loc - m_new)
    ct_ref[h] = (jnp.concatenate([a_dec, a_dec], axis=1) * ct_ref[h]
                 + jnp.concatenate([s_dec, s_dec], axis=1) * s_loc)
    n_ref[h:h + 1, :] = a_dec * n_ref[h:h + 1, :] + s_dec * n_loc
    m_ref[h:h + 1, :] = m_new


def _head_expand(act, cum, expand_ref):
    L = act.shape[0]
    lane = lax.broadcasted_iota(jnp.int32, (L, GATE_W), 1)
    is_dt = (lane >= GDT0) & (lane < GEND)
    tot = cum[L - 1:L, :]
    e_cum = jnp.where(is_dt, jnp.exp2(cum), 0.0)
    e_end = jnp.where(is_dt, jnp.exp2(tot - cum) * act, 0.0)

    def split(x):
        hi = x.astype(BF16)
        lo = (x - hi.astype(F32)).astype(BF16)
        return jnp.concatenate([hi, lo], axis=1)

    lhs = jnp.concatenate([split(e_cum), split(e_end)], axis=0)
    return jnp.dot(lhs, expand_ref[...], preferred_element_type=F32)


def _chunk_step(r0, big_ref, gs, p, ct_ref, n_ref, m_ref, st_ref, hm_ref, ys_ref, hmask):
    L = CHUNK
    rows = pl.ds(r0, L)
    row_i = lax.broadcasted_iota(jnp.int32, (L, L), 0)
    col_i = lax.broadcasted_iota(jnp.int32, (L, L), 1)
    causal = col_i <= row_i
    neg_inf = -jnp.inf

    rep_ref = gs["rep"]
    act_t = gs["act_t"][rows, :]
    cum_t = gs["cum_t"][rows, :]
    e_rows = pl.multiple_of(2 * r0, 2 * L)

    zero_blk = jnp.zeros((L, LANE), BF16)

    def paired_nt(lhs_pair, rhs0, rhs1):
        rhs = jnp.concatenate([jnp.concatenate([rhs0, zero_blk], axis=1),
                               jnp.concatenate([zero_blk, rhs1], axis=1)], axis=0)
        out = lax.dot_general(lhs_pair, rhs, (((1,), (1,)), ((), ())), preferred_element_type=F32)
        return out[:, :LANE], out[:, LANE:]

    qk_heads = []
    for pr in range(M_HEADS // 2):
        qk_heads += paired_nt(
            big_ref[rows, Q0 + 2 * pr * M_DQK:Q0 + 2 * (pr + 1) * M_DQK],
            big_ref[rows, K0 + 2 * pr * M_DQK:K0 + (2 * pr + 1) * M_DQK],
            big_ref[rows, K0 + (2 * pr + 1) * M_DQK:K0 + 2 * (pr + 1) * M_DQK])

    for h in range(M_HEADS):
        q = big_ref[rows, Q0 + h * M_DQK:Q0 + (h + 1) * M_DQK]
        k = big_ref[rows, K0 + h * M_DQK:K0 + (h + 1) * M_DQK]
        v = big_ref[rows, V0 + h * M_DV:V0 + (h + 1) * M_DV]
        gi, gf = GI0 + h, GF0 + h
        bt_b = rep_ref[h, rows, :]
        it_b = rep_ref[M_HEADS + h, rows, :]
        bt_r = cum_t[gf:gf + 1, :]
        it_r = act_t[gi:gi + 1, :]
        b_tot = bt_b[L - 1:L, :]
        m_prev = m_ref[h:h + 1, :]
        qk = qk_heads[h]
        d_log = jnp.where(causal, bt_b - bt_r + it_r, neg_inf)
        mx = jnp.max(d_log, axis=1, keepdims=True)
        inter_log = bt_b + m_prev
        m_t = jnp.maximum(inter_log, mx)
        w_ts = jnp.exp2(d_log - m_t) * qk
        inter = jnp.exp2(inter_log - m_t)
        q_f = q.astype(F32)
        lhs = jnp.concatenate([w_ts.astype(BF16), (q_f * inter).astype(BF16)], axis=1)
        rhs = jnp.concatenate([v, ct_ref[h].astype(BF16)], axis=0)
        num = jnp.dot(lhs, rhs, preferred_element_type=F32)
        q_n = jnp.sum(q_f * n_ref[h:h + 1, :], axis=1, keepdims=True)
        den = jnp.sum(w_ts, axis=1, keepdims=True) + inter * q_n
        denom = jnp.maximum(jnp.abs(den), jnp.exp2(-m_t))
        ms = jnp.mean(num * num, axis=1, keepdims=True)
        rs = lax.rsqrt(ms + EPS * (denom * denom))
        y = num * jnp.concatenate([rs, rs], axis=1) * p["mng"][:, h * M_DV:(h + 1) * M_DV]
        o_gate = big_ref[rows, O0 + h * M_DV:O0 + (h + 1) * M_DV]
        hm_ref[rows, h * M_DV:(h + 1) * M_DV] = y.astype(BF16) * o_gate
        _mlstm_state(k, v, bt_b, it_b, b_tot, m_prev, ct_ref, n_ref, m_ref, h)

    e_ref = gs["expanded"]
    cb_groups = []
    for pr in range(S_GROUPS // 2):
        cb_groups += paired_nt(
            big_ref[rows, X0 + XC0 + 2 * pr * S_STATE:X0 + XC0 + 2 * (pr + 1) * S_STATE],
            big_ref[rows, X0 + XB0 + 2 * pr * S_STATE:X0 + XB0 + (2 * pr + 1) * S_STATE],
            big_ref[rows, X0 + XB0 + (2 * pr + 1) * S_STATE:X0 + XB0 + 2 * (pr + 1) * S_STATE])
    for g in range(S_GROUPS):
        gcols = slice(g * S_GW, (g + 1) * S_GW)
        bg = big_ref[rows, X0 + XB0 + g * S_STATE:X0 + XB0 + (g + 1) * S_STATE]
        cg = big_ref[rows, X0 + XC0 + g * S_STATE:X0 + XC0 + (g + 1) * S_STATE]
        xg_b = big_ref[rows, X0 + g * S_GW:X0 + (g + 1) * S_GW]
        xg = xg_b.astype(F32)
        cbm = cb_groups[g]
        w_heads = []
        for j in range(S_HPG):
            gk = GDT0 + g * S_HPG + j
            ca_b = rep_ref[2 * M_HEADS + g * S_HPG + j, rows, :]
            ca_r = cum_t[gk:gk + 1, :]
            dt_r = act_t[gk:gk + 1, :]
            dec = jnp.exp2(jnp.where(causal, ca_b - ca_r, neg_inf))
            w_heads.append((cbm * dec * dt_r).astype(BF16))
        y_diag = jnp.zeros((L, S_GW), F32)
        for j in range(0, S_HPG, 2):
            y_diag = y_diag + jnp.dot(
                jnp.concatenate([w_heads[j], w_heads[j + 1]], axis=1),
                jnp.concatenate([xg_b * hmask[j], xg_b * hmask[j + 1]], axis=0),
                preferred_element_type=F32)
        e_cum = e_ref[pl.ds(e_rows, L), gcols]
        e_end = e_ref[pl.ds(e_rows + L, L), gcols]
        e_tot = e_ref[pl.ds(e_rows + L - SUBLANE, SUBLANE), gcols][SUBLANE - 1:SUBLANE, :]
        y_off = jnp.dot(cg, st_ref[g].astype(BF16), preferred_element_type=F32) * e_cum
        y = y_diag + y_off + p["sd"][:, gcols] * xg
        yz = y * big_ref[rows, Z0 + g * S_GW:Z0 + (g + 1) * S_GW].astype(F32)
        ms = jnp.mean(yz * yz, axis=1, keepdims=True)
        ys_ref[rows, gcols] = (yz * lax.rsqrt(ms + EPS) * p["sng"][:, gcols]).astype(BF16)
        xw = (xg * e_end).astype(BF16)
        s_new = lax.dot_general(bg, xw, (((0,), (0,)), ((), ())), preferred_element_type=F32)
        st_ref[g] = st_ref[g] * e_tot + s_new


_SMALL_KEYS = ("gbias", "alog", "mng", "sd", "sng", "expand")


def _init_kernel(big_ref, gate_ref, gbias_ref, alog_ref, cw_ref, cb_ref, expand_ref,
                 ct_ref, n_ref, m_ref, st_ref, xh_ref):
    L = CHUNK
    act, cum = _gate_block(gate_ref[...] + gbias_ref[...], alog_ref[...], True, _tril3(L))
    ct_ref[...] = jnp.zeros_like(ct_ref)
    n_ref[...] = jnp.zeros_like(n_ref)
    m_ref[...] = jnp.zeros_like(m_ref)
    for h in range(M_HEADS):
        k = big_ref[:, K0 + h * M_DQK:K0 + (h + 1) * M_DQK]
        v = big_ref[:, V0 + h * M_DV:V0 + (h + 1) * M_DV]
        gi, gf = GI0 + h, GF0 + h
        bt_b = _lane_rep(cum, gf)
        _mlstm_state(k, v, bt_b, _lane_rep(act, gi), bt_b[L - 1:L, :], m_ref[h:h + 1, :],
                     ct_ref, n_ref, m_ref, h)

    e_all = _head_expand(act, cum, expand_ref)
    valid = lax.broadcasted_iota(jnp.int32, (L, S_XBC), 0) >= (L - N_META)
    xa = jnp.where(valid, big_ref[:, X0:X0 + S_XBC].astype(F32), 0.0)
    xh_ref[...] = xa[L - SUBLANE:L, :]
    xc = _conv_silu(xa, jnp.zeros((SUBLANE, S_XBC), F32), cw_ref, cb_ref, slice(0, S_XBC))
    for g in range(S_GROUPS):
        gcols = slice(g * S_GW, (g + 1) * S_GW)
        bg = xc[:, XB0 + g * S_STATE:XB0 + (g + 1) * S_STATE].astype(BF16)
        xw = (xc[:, gcols] * e_all[L:2 * L, gcols]).astype(BF16)
        st_ref[g] = lax.dot_general(bg, xw, (((0,), (0,)), ((), ())), preferred_element_type=F32)


_STATE_SHAPES = (
    (M_HEADS, M_DQK, M_DV),
    (SUBLANE, M_DQK),
    (SUBLANE, LANE),
    (S_GROUPS, S_STATE, S_GW),
    (SUBLANE, S_XBC),
)


def _init_state(big_m, gate_m, consts):
    return pl.pallas_call(
        _init_kernel,
        grid=(1,),
        in_specs=[_const_spec(big_m.shape), _const_spec(gate_m.shape)] + [_const_spec(a.shape) for a in consts],
        out_specs=[_const_spec(s) for s in _STATE_SHAPES],
        out_shape=[jax.ShapeDtypeStruct(s, F32) for s in _STATE_SHAPES],
        compiler_params=pltpu.CompilerParams(
            dimension_semantics=("arbitrary",), vmem_limit_bytes=VMEM_LIMIT),
        name="init_state",
    )(big_m, gate_m, *consts)


def _mixer_kernel(big_ref, gate_ref, x_ref, *refs):
    ns = len(_SMALL_KEYS)
    p = dict(zip(_SMALL_KEYS, refs[:ns]))
    ct0, n0, m0, st0 = refs[ns:ns + 4]
    mproj_ref, sproj_ref, wout_ref = refs[ns + 4:ns + 7]
    h1_ref = refs[ns + 7]
    ct_ref, n_ref, m_ref, st_ref, hm_ref, ys_ref, mg_ref = refs[ns + 8:ns + 15]
    gs = dict(zip(("rep", "act_t", "cum_t", "expanded"), refs[ns + 15:]))

    @pl.when(pl.program_id(1) == 0)
    def _():
        ct_ref[...] = ct0[...]
        n_ref[...] = n0[...]
        m_ref[...] = m0[...]
        st_ref[...] = st0[...]

    tril3 = _tril3(CHUNK)
    for c in range(CPT):
        rs = slice(c * CHUNK, (c + 1) * CHUNK)
        act, cum = _gate_block(gate_ref[rs, :] + p["gbias"][...], p["alog"][...], False, tril3)
        for h in range(M_HEADS):
            gs["rep"][h, rs, :] = _lane_rep(cum, GF0 + h)
            gs["rep"][M_HEADS + h, rs, :] = _lane_rep(act, GI0 + h)
        for j in range(S_HEADS):
            gs["rep"][2 * M_HEADS + j, rs, :] = _lane_rep(cum, GDT0 + j)
        gs["act_t"][rs, :] = act.T
        gs["cum_t"][rs, :] = cum.T
        gs["expanded"][2 * c * CHUNK:2 * (c + 1) * CHUNK, :] = _head_expand(act, cum, p["expand"])

    lane_g = lax.broadcasted_iota(jnp.int32, (CHUNK, S_GW), 1)
    hmask = [((lane_g >= j * S_HEADDIM) & (lane_g < (j + 1) * S_HEADDIM)).astype(F32).astype(BF16)
             for j in range(S_HPG)]

    def body(c, carry):
        r0 = pl.multiple_of(c * CHUNK, CHUNK)
        _chunk_step(r0, big_ref, gs, p, ct_ref, n_ref, m_ref, st_ref, hm_ref, ys_ref, hmask)
        return carry

    def project(rs):
        for nb in range(D_MODEL // FB):
            cols = slice(nb * FB, (nb + 1) * FB)
            br_a = jnp.dot(hm_ref[rs, :], mproj_ref[:, cols], preferred_element_type=F32)
            br_b = jnp.dot(ys_ref[rs, :], sproj_ref[:, cols], preferred_element_type=F32)
            ga = big_ref[rs, GA0 + nb * FB:GA0 + (nb + 1) * FB].astype(F32)
            gb = big_ref[rs, GB0 + nb * FB:GB0 + (nb + 1) * FB].astype(F32)
            mg_ref[rs, cols] = (ga * br_a + gb * br_b).astype(BF16)
        for nb in range(D_MODEL // FB):
            cols = slice(nb * FB, (nb + 1) * FB)
            h1_ref[rs, cols] = x_ref[rs, cols] + jnp.dot(mg_ref[rs, :], wout_ref[:, cols], preferred_element_type=F32)

    for c in range(CPT // PROJ_CHUNKS):
        for cc in range(PROJ_CHUNKS):
            body(c * PROJ_CHUNKS + cc, 0)
        project(slice(c * PROJ_CHUNKS * CHUNK, (c + 1) * PROJ_CHUNKS * CHUNK))


def _mixer(big, gate, x2, small, init, mproj, sproj, wout, batch):
    n = x2.shape[0]
    tm = MIX_TM
    tiles = n // batch // tm
    row_map = lambda b, j: (b * tiles + j, 0)
    state_shapes = _STATE_SHAPES[:4]
    return pl.pallas_call(
        _mixer_kernel,
        grid=(batch, tiles),
        in_specs=[
            pl.BlockSpec((tm, BIG_W), row_map),
            pl.BlockSpec((tm, GATE_W), row_map),
            pl.BlockSpec((tm, D_MODEL), row_map),
        ] + [_const_spec(s.shape) for s in small]
          + [_const_spec(s) for s in state_shapes]
          + [_const_spec((D_MODEL, D_MODEL))] * 3,
        out_specs=pl.BlockSpec((tm, D_MODEL), row_map),
        out_shape=jax.ShapeDtypeStruct((n, D_MODEL), F32),
        scratch_shapes=[pltpu.VMEM(s, F32) for s in state_shapes] + [
            pltpu.VMEM((tm, M_V), BF16),
            pltpu.VMEM((tm, S_INNER), BF16),
            pltpu.VMEM((tm, D_MODEL), BF16),
            pltpu.VMEM((2 * M_HEADS + S_HEADS, tm, LANE), F32),
            pltpu.VMEM((tm, GATE_W), F32),
            pltpu.VMEM((tm, GATE_W), F32),
            pltpu.VMEM((2 * tm, S_INNER), F32),
        ],
        compiler_params=pltpu.CompilerParams(
            dimension_semantics=("arbitrary", "arbitrary"), vmem_limit_bytes=VMEM_LIMIT),
        name="mixer",
    )(big, gate, x2, *small, *init[:4], mproj, sproj, wout)


def _ffn_kernel(h_ref, g2_ref, gf_ref, w1_ref, w2_ref, o_ref, u_ref, hid_ref):
    h = h_ref[...]
    ms = jnp.mean(h * h, axis=-1, keepdims=True)
    u_ref[...] = (h * lax.rsqrt(ms + EPS) * g2_ref[...]).astype(BF16)
    for jb in range(D_FF // FB):
        gate = jnp.dot(u_ref[...], w1_ref[:, jb * FB:(jb + 1) * FB], preferred_element_type=F32)
        up = jnp.dot(u_ref[...], w1_ref[:, D_FF + jb * FB:D_FF + (jb + 1) * FB], preferred_element_type=F32)
        hid_ref[:, jb * FB:(jb + 1) * FB] = (gate * jax.nn.sigmoid(gate) * up).astype(BF16)
    for nb in range(D_MODEL // FB):
        cols = slice(nb * FB, (nb + 1) * FB)
        o_ref[:, cols] = h_ref[:, cols] + jnp.dot(hid_ref[...], w2_ref[:, cols], preferred_element_type=F32)
    h2 = o_ref[...]
    ms2 = jnp.mean(h2 * h2, axis=-1, keepdims=True)
    o_ref[...] = h2 * lax.rsqrt(ms2 + EPS) * gf_ref[...]


def _ffn(h1, g2, gf, w1, w2):
    n = h1.shape[0]
    tm = FFN_TM
    return pl.pallas_call(
        _ffn_kernel,
        grid=(n // tm,),
        in_specs=[
            pl.BlockSpec((tm, D_MODEL), lambda i: (i, 0)),
            _const_spec((1, D_MODEL)),
            _const_spec((1, D_MODEL)),
            _const_spec((D_MODEL, 2 * D_FF)),
            _const_spec((D_FF, D_MODEL)),
        ],
        out_specs=pl.BlockSpec((tm, D_MODEL), lambda i: (i, 0)),
        out_shape=jax.ShapeDtypeStruct((n, D_MODEL), F32),
        scratch_shapes=[pltpu.VMEM((tm, D_MODEL), BF16), pltpu.VMEM((tm, D_FF), BF16)],
        compiler_params=pltpu.CompilerParams(
            dimension_semantics=("arbitrary",), vmem_limit_bytes=VMEM_LIMIT),
        name="ffn",
    )(h1, g2, gf, w1, w2)


def _pad_lanes(row, width):
    return jnp.pad(row, ((0, 0), (0, width - row.shape[1])))


def kernel(x, meta, norm1_g, w_in, m_igate_b, m_fgate_b, m_norm_g, m_proj, s_conv_w, s_conv_b,
           s_dt_bias, s_A_log, s_D, s_norm_g, s_proj, w_out, norm2_g, w_ffn_in, w_ffn_out, norm_f_g):
    bsz, seq, dm = x.shape
    assert dm == D_MODEL and seq % TM == 0 and w_in.shape[0] == 1
    w_all = _pack_w_in(w_in[0].T)

    gbias = _pad_lanes(jnp.concatenate([m_igate_b[0], m_fgate_b[0], s_dt_bias[0]])[None].astype(F32), GATE_W)
    alog = _pad_lanes(jnp.concatenate([jnp.zeros((GDT0,), F32), s_A_log[0].astype(F32)])[None], GATE_W)
    conv_w = jnp.pad(s_conv_w[0].astype(F32), ((0, SUBLANE - S_CONV), (0, 0)))
    conv_b = s_conv_b[0][None].astype(F32)
    head_of_lane = jnp.arange(S_INNER) // S_HEADDIM
    expand1 = (jnp.arange(GATE_W)[:, None] == GDT0 + head_of_lane[None, :]).astype(BF16)
    expand = jnp.concatenate([expand1, expand1], axis=0)
    small = (
        gbias, alog,
        m_norm_g[0].reshape(1, M_V).astype(F32),
        jnp.repeat(s_D[0].astype(F32), S_HEADDIM)[None],
        s_norm_g[0].reshape(1, S_INNER).astype(F32),
        expand,
    )
    g1 = norm1_g[0][None].astype(F32)

    meta_rows = jnp.concatenate([jnp.zeros((CHUNK - N_META, D_MODEL), x.dtype), meta.astype(x.dtype)], axis=0)
    big_m, gate_m = _inproj(meta_rows, g1, w_all, CHUNK)
    init = _init_state(big_m, gate_m, (gbias, alog, conv_w, conv_b, expand))

    x2 = x.reshape(bsz * seq, dm)
    big, gate = _inproj(x2, g1, w_all, TM, conv=(init[4], conv_w, conv_b), batch=bsz)
    h1 = _mixer(big, gate, x2, small, init, m_proj[0].astype(BF16), s_proj[0].astype(BF16),
                w_out[0].astype(BF16), bsz)
    out = _ffn(h1, norm2_g[0][None].astype(F32), norm_f_g[None].astype(F32),
               w_ffn_in[0].astype(BF16), w_ffn_out[0].astype(BF16))
    return out.reshape(bsz, seq, dm)
```

```python
import math

import jax
import jax.numpy as jnp
from jax import lax
from jax.experimental import pallas as pl
from jax.experimental.pallas import tpu as pltpu

F32 = jnp.float32
BF16 = jnp.bfloat16
LOG2E = math.log2(math.e)

D_MODEL = 1024
N_META = 16
CHUNK = 128
EPS = 1e-6

M_HEADS = 4
M_DQK = 128
M_DV = 256
M_QK = M_HEADS * M_DQK
M_V = M_HEADS * M_DV
GATE_CAP = 15.0

S_HEADDIM = 64
S_INNER = D_MODEL
S_HEADS = S_INNER // S_HEADDIM
S_GROUPS = 4
S_HPG = S_HEADS // S_GROUPS
S_STATE = 128
S_CONV = 4
S_GW = S_HPG * S_HEADDIM
S_XBC = S_INNER + 2 * S_GROUPS * S_STATE
XB0 = S_INNER
XC0 = S_INNER + S_GROUPS * S_STATE

D_FF = 2816

Q0 = 0
K0 = Q0 + M_QK
V0 = K0 + M_QK
O0 = V0 + M_V
Z0 = O0 + M_V
X0 = Z0 + S_INNER
GA0 = X0 + S_XBC
GB0 = GA0 + D_MODEL
BIG_W = GB0 + D_MODEL
GATE_W = 128
GI0, GF0, GDT0, GEND = 0, M_HEADS, 2 * M_HEADS, 2 * M_HEADS + S_HEADS

LANE = 128
SUBLANE = 8
TM = 512
MIX_TM = 512
FFN_TM = 1024
CPT = MIX_TM // CHUNK
PROJ_CHUNKS = 2
NB = 512
FB = 256
VMEM_LIMIT = 56 * 1024 * 1024


def _softplus(x):
    return jnp.maximum(x, 0.0) + jnp.log1p(jnp.exp(-jnp.abs(x)))


def _const_spec(shape):
    nd = len(shape)
    return pl.BlockSpec(shape, lambda *_: (0,) * nd, pipeline_mode=pl.Buffered(1))


def _conv_silu(xa, hist, cw_ref, cb_ref, cols):
    row8 = lax.broadcasted_iota(jnp.int32, (SUBLANE, xa.shape[1]), 0)
    acc = cb_ref[:, cols] + xa * cw_ref[S_CONV - 1:S_CONV, cols]
    for sh in range(1, S_CONV):
        rolled = pltpu.roll(xa, sh, axis=0)
        top = jnp.where(row8 < sh, pltpu.roll(hist, sh, axis=0), rolled[0:SUBLANE])
        shifted = jnp.concatenate([top, rolled[SUBLANE:]], axis=0)
        acc = acc + shifted * cw_ref[S_CONV - 1 - sh:S_CONV - sh, cols]
    return acc * jax.nn.sigmoid(acc)


_IN_SIZES = (M_QK, M_QK, M_V, M_V, M_HEADS, M_HEADS, S_INNER, S_XBC, S_HEADS, D_MODEL, D_MODEL)
_IN_OFFS = tuple(sum(_IN_SIZES[:i]) for i in range(len(_IN_SIZES) + 1))
IN_WIDTH = _IN_OFFS[-1]
_PACK_SEGMENTS = (
    (_IN_OFFS[0], _IN_OFFS[4], Q0),
    (_IN_OFFS[6], _IN_OFFS[7], Z0),
    (_IN_OFFS[7], _IN_OFFS[8], X0),
    (_IN_OFFS[9], _IN_OFFS[10], GA0),
    (_IN_OFFS[10], _IN_OFFS[11], GB0),
)
PACK_ROWS = 512


def _pack_kernel(wt_ref, o_ref):
    for s0, s1, d0 in _PACK_SEGMENTS:
        for r in range(s0, s1, PACK_ROWS):
            o_ref[:, d0 + r - s0:d0 + r - s0 + PACK_ROWS] = wt_ref[r:r + PACK_ROWS, :].T.astype(BF16)
    kb = wt_ref.shape[1]
    gates = jnp.concatenate(
        [wt_ref[_IN_OFFS[4]:_IN_OFFS[6], :], wt_ref[_IN_OFFS[8]:_IN_OFFS[9], :],
         jnp.zeros((GATE_W - GEND, kb), F32)], axis=0)
    o_ref[:, BIG_W:BIG_W + GATE_W] = gates.T.astype(BF16)


def _pack_w_in(wt):
    kb = LANE
    return pl.pallas_call(
        _pack_kernel,
        grid=(D_MODEL // kb,),
        in_specs=[pl.BlockSpec((IN_WIDTH, kb), lambda i: (0, i))],
        out_specs=pl.BlockSpec((kb, BIG_W + GATE_W), lambda i: (i, 0)),
        out_shape=jax.ShapeDtypeStruct((D_MODEL, BIG_W + GATE_W), BF16),
        compiler_params=pltpu.CompilerParams(
            dimension_semantics=("arbitrary",), vmem_limit_bytes=VMEM_LIMIT),
        name="pack_w_in",
    )(wt)


def _inproj_kernel(x_ref, g_ref, w_ref, meta_ref, cw_ref, cb_ref,
                   big_ref, gate_ref, bigm_ref, gatem_ref, u_ref, cs_ref, xh0_ref):
    tm = x_ref.shape[0]

    def normed(rows):
        ms = jnp.mean(rows * rows, axis=-1, keepdims=True)
        return (rows * lax.rsqrt(ms + EPS) * g_ref[...]).astype(BF16)

    @pl.when((pl.program_id(0) == 0) & (pl.program_id(1) == 0))
    def _():
        um = normed(meta_ref[...])
        bigm_ref[...] = jnp.zeros_like(bigm_ref)
        for c0 in list(range(K0, O0, NB)) + list(range(X0, GA0, NB)):
            acc = jnp.dot(um, w_ref[:, c0:c0 + NB], preferred_element_type=F32)
            bigm_ref[:, c0:c0 + NB] = acc.astype(BF16)
            if c0 >= X0:
                xh0_ref[:, c0 - X0:c0 - X0 + NB] = acc[CHUNK - SUBLANE:CHUNK, :]
        gatem_ref[...] = jnp.dot(um, w_ref[:, BIG_W:BIG_W + GATE_W], preferred_element_type=F32)

    @pl.when(pl.program_id(1) == 0)
    def _():
        for s in range(S_XBC // LANE):
            cs_ref[s, 0:SUBLANE, :] = xh0_ref[:, s * LANE:(s + 1) * LANE]

    u_ref[...] = normed(x_ref[...])
    for n in range(BIG_W // NB):
        c0 = n * NB
        acc = jnp.dot(u_ref[...], w_ref[:, c0:c0 + NB], preferred_element_type=F32)
        if c0 < K0:
            acc = acc * (M_DQK ** -0.5)
        elif O0 <= c0 < Z0 or c0 >= GA0:
            acc = jax.nn.sigmoid(acc)
        elif Z0 <= c0 < X0:
            acc = acc * jax.nn.sigmoid(acc)
        elif X0 <= c0 < GA0:
            for s in range(NB // LANE):
                slab = (c0 - X0) // LANE + s
                cols = slice((c0 - X0) + s * LANE, (c0 - X0) + (s + 1) * LANE)
                cs_ref[slab, SUBLANE:SUBLANE + tm, :] = acc[:, s * LANE:(s + 1) * LANE]
                half = 0.5 * cb_ref[:, cols]
                for tap in range(S_CONV):
                    back = S_CONV - 1 - tap
                    half = half + (cs_ref[slab, pl.ds(SUBLANE - back, tm, stride=1), :]
                                   * (0.5 * cw_ref[tap:tap + 1, cols]))
                cs_ref[slab, 0:SUBLANE, :] = cs_ref[slab, tm:tm + SUBLANE, :]
                big_ref[:, c0 + s * LANE:c0 + (s + 1) * LANE] = (half + half * jnp.tanh(half)).astype(BF16)
            continue
        big_ref[:, c0:c0 + NB] = acc.astype(BF16)
    gate_ref[...] = jnp.dot(u_ref[...], w_ref[:, BIG_W:BIG_W + GATE_W], preferred_element_type=F32)


def _inproj(rows, g, w_all, meta_rows, conv_w, conv_b, batch):
    n = rows.shape[0]
    tm = TM
    tiles = n // batch // tm
    row_map = lambda b, j: (b * tiles + j, 0)
    consts = (g, w_all, meta_rows, conv_w, conv_b)
    return pl.pallas_call(
        _inproj_kernel,
        grid=(batch, tiles),
        in_specs=[pl.BlockSpec((tm, D_MODEL), row_map)] + [_const_spec(a.shape) for a in consts],
        out_specs=[
            pl.BlockSpec((tm, BIG_W), row_map),
            pl.BlockSpec((tm, GATE_W), row_map),
            _const_spec((CHUNK, BIG_W)),
            _const_spec((CHUNK, GATE_W)),
        ],
        out_shape=[
            jax.ShapeDtypeStruct((n, BIG_W), BF16),
            jax.ShapeDtypeStruct((n, GATE_W), F32),
            jax.ShapeDtypeStruct((CHUNK, BIG_W), BF16),
            jax.ShapeDtypeStruct((CHUNK, GATE_W), F32),
        ],
        scratch_shapes=[
            pltpu.VMEM((tm, D_MODEL), BF16),
            pltpu.VMEM((S_XBC // LANE, SUBLANE + tm, LANE), F32),
            pltpu.VMEM((SUBLANE, S_XBC), F32),
        ],
        compiler_params=pltpu.CompilerParams(
            dimension_semantics=("arbitrary", "arbitrary"), vmem_limit_bytes=VMEM_LIMIT),
        name="inproj",
    )(rows, *consts)


def _gate_block(pre, alog_row, masked, tril3):
    L = pre.shape[0]
    lane = lax.broadcasted_iota(jnp.int32, (L, GATE_W), 1)
    sc = GATE_CAP * jnp.tanh(pre / GATE_CAP)
    i_log = sc * LOG2E
    f_log = -_softplus(-sc) * LOG2E
    dt = _softplus(pre)
    if masked:
        valid = lax.broadcasted_iota(jnp.int32, (L, GATE_W), 0) >= (L - N_META)
        i_log = jnp.where(valid, i_log, -jnp.inf)
        f_log = jnp.where(valid, f_log, 0.0)
        dt = jnp.where(valid, dt, 0.0)
    act = jnp.where(lane < GF0, i_log, jnp.where(lane < GDT0, f_log, jnp.where(lane < GEND, dt, 0.0)))
    a_row = -jnp.exp(alog_row) * LOG2E
    cs = jnp.where(lane < GF0, 0.0, jnp.where(lane < GDT0, act, jnp.where(lane < GEND, act * a_row, 0.0)))
    hi = cs.astype(BF16)
    rem = cs - hi.astype(F32)
    mid = rem.astype(BF16)
    lo = (rem - mid.astype(F32)).astype(BF16)
    cum = jnp.dot(tril3, jnp.concatenate([hi, mid, lo], axis=0), preferred_element_type=F32)
    return act, cum


def _tril3(L):
    assert L & (L - 1) == 0
    row_i = lax.broadcasted_iota(jnp.int32, (L, 3 * L), 0)
    col_i = lax.broadcasted_iota(jnp.int32, (L, 3 * L), 1)
    return (jnp.bitwise_and(col_i, L - 1) <= row_i).astype(F32).astype(BF16)


def _lane_rep(x, col):
    return jnp.broadcast_to(x[:, col:col + 1], (x.shape[0], LANE))


def _mlstm_state(k, v, bt_b, it_b, b_tot, m_prev, ct_ref, n_ref, m_ref, h):
    w_end = b_tot - bt_b + it_b
    m_loc = jnp.max(w_end, axis=0, keepdims=True)
    wgt = jnp.exp2(w_end - m_loc)
    kw = k.astype(F32) * wgt
    s_loc = lax.dot_general(kw.astype(BF16), v, (((0,), (0,)), ((), ())), preferred_element_type=F32)
    n_loc = jnp.sum(kw, axis=0, keepdims=True)
    m_new = jnp.maximum(b_tot + m_prev, m_loc)
    a_dec = jnp.exp2(b_tot + m_prev - m_new)
    s_dec = jnp.exp2(m_loc - m_new)
    ct_ref[h] = (jnp.concatenate([a_dec, a_dec], axis=1) * ct_ref[h]
                 + jnp.concatenate([s_dec, s_dec], axis=1) * s_loc)
    n_ref[h:h + 1, :] = a_dec * n_ref[h:h + 1, :] + s_dec * n_loc
    m_ref[h:h + 1, :] = m_new


def _head_expand(act, cum, expand_ref):
    L = act.shape[0]
    lane = lax.broadcasted_iota(jnp.int32, (L, GATE_W), 1)
    is_dt = (lane >= GDT0) & (lane < GEND)
    tot = cum[L - 1:L, :]
    e_cum = jnp.where(is_dt, jnp.exp2(cum), 0.0)
    e_end = jnp.where(is_dt, jnp.exp2(tot - cum) * act, 0.0)

    def split(x):
        hi = x.astype(BF16)
        lo = (x - hi.astype(F32)).astype(BF16)
        return jnp.concatenate([hi, lo], axis=1)

    lhs = jnp.concatenate([split(e_cum), split(e_end)], axis=0)
    return jnp.dot(lhs, expand_ref[...], preferred_element_type=F32)


def _chunk_step(r0, big_ref, gs, p, ct_ref, n_ref, m_ref, st_ref, hm_ref, ys_ref, hmask):
    L = CHUNK
    rows = pl.ds(r0, L)
    row_i = lax.broadcasted_iota(jnp.int32, (L, L), 0)
    col_i = lax.broadcasted_iota(jnp.int32, (L, L), 1)
    causal = col_i <= row_i
    neg_inf = -jnp.inf

    rep_ref = gs["rep"]
    act_t = gs["act_t"][rows, :]
    cum_t = gs["cum_t"][rows, :]
    e_rows = pl.multiple_of(2 * r0, 2 * L)

    zero_blk = jnp.zeros((L, LANE), BF16)

    def paired_nt(lhs_pair, rhs0, rhs1):
        rhs = jnp.concatenate([jnp.concatenate([rhs0, zero_blk], axis=1),
                               jnp.concatenate([zero_blk, rhs1], axis=1)], axis=0)
        out = lax.dot_general(lhs_pair, rhs, (((1,), (1,)), ((), ())), preferred_element_type=F32)
        return out[:, :LANE], out[:, LANE:]

    qk_heads = []
    for pr in range(M_HEADS // 2):
        qk_heads += paired_nt(
            big_ref[rows, Q0 + 2 * pr * M_DQK:Q0 + 2 * (pr + 1) * M_DQK],
            big_ref[rows, K0 + 2 * pr * M_DQK:K0 + (2 * pr + 1) * M_DQK],
            big_ref[rows, K0 + (2 * pr + 1) * M_DQK:K0 + 2 * (pr + 1) * M_DQK])

    for h in range(M_HEADS):
        q = big_ref[rows, Q0 + h * M_DQK:Q0 + (h + 1) * M_DQK]
        k = big_ref[rows, K0 + h * M_DQK:K0 + (h + 1) * M_DQK]
        v = big_ref[rows, V0 + h * M_DV:V0 + (h + 1) * M_DV]
        gi, gf = GI0 + h, GF0 + h
        bt_b = rep_ref[h, rows, :]
        it_b = rep_ref[M_HEADS + h, rows, :]
        bt_r = cum_t[gf:gf + 1, :]
        it_r = act_t[gi:gi + 1, :]
        b_tot = bt_b[L - 1:L, :]
        m_prev = m_ref[h:h + 1, :]
        qk = qk_heads[h]
        d_log = jnp.where(causal, bt_b - bt_r + it_r, neg_inf)
        mx = jnp.max(d_log, axis=1, keepdims=True)
        inter_log = bt_b + m_prev
        m_t = jnp.maximum(inter_log, mx)
        w_ts = jnp.exp2(d_log - m_t) * qk
        inter = jnp.exp2(inter_log - m_t)
        q_f = q.astype(F32)
        lhs = jnp.concatenate([w_ts.astype(BF16), (q_f * inter).astype(BF16)], axis=1)
        rhs = jnp.concatenate([v, ct_ref[h].astype(BF16)], axis=0)
        num = jnp.dot(lhs, rhs, preferred_element_type=F32)
        q_n = jnp.sum(q_f * n_ref[h:h + 1, :], axis=1, keepdims=True)
        den = jnp.sum(w_ts, axis=1, keepdims=True) + inter * q_n
        denom = jnp.maximum(jnp.abs(den), jnp.exp2(-m_t))
        ms = jnp.mean(num * num, axis=1, keepdims=True)
        rs = lax.rsqrt(ms + EPS * (denom * denom))
        y = num * jnp.concatenate([rs, rs], axis=1) * p["mng"][:, h * M_DV:(h + 1) * M_DV]
        o_gate = big_ref[rows, O0 + h * M_DV:O0 + (h + 1) * M_DV]
        hm_ref[rows, h * M_DV:(h + 1) * M_DV] = y.astype(BF16) * o_gate
        _mlstm_state(k, v, bt_b, it_b, b_tot, m_prev, ct_ref, n_ref, m_ref, h)

    e_ref = gs["expanded"]
    cb_groups = []
    for pr in range(S_GROUPS // 2):
        cb_groups += paired_nt(
            big_ref[rows, X0 + XC0 + 2 * pr * S_STATE:X0 + XC0 + 2 * (pr + 1) * S_STATE],
            big_ref[rows, X0 + XB0 + 2 * pr * S_STATE:X0 + XB0 + (2 * pr + 1) * S_STATE],
            big_ref[rows, X0 + XB0 + (2 * pr + 1) * S_STATE:X0 + XB0 + 2 * (pr + 1) * S_STATE])
    for g in range(S_GROUPS):
        gcols = slice(g * S_GW, (g + 1) * S_GW)
        bg = big_ref[rows, X0 + XB0 + g * S_STATE:X0 + XB0 + (g + 1) * S_STATE]
        cg = big_ref[rows, X0 + XC0 + g * S_STATE:X0 + XC0 + (g + 1) * S_STATE]
        xg_b = big_ref[rows, X0 + g * S_GW:X0 + (g + 1) * S_GW]
        xg = xg_b.astype(F32)
        cbm = cb_groups[g]
        w_heads = []
        for j in range(S_HPG):
            gk = GDT0 + g * S_HPG + j
            ca_b = rep_ref[2 * M_HEADS + g * S_HPG + j, rows, :]
            ca_r = cum_t[gk:gk + 1, :]
            dt_r = act_t[gk:gk + 1, :]
            dec = jnp.exp2(jnp.where(causal, ca_b - ca_r, neg_inf))
            w_heads.append((cbm * dec * dt_r).astype(BF16))
        y_diag = jnp.zeros((L, S_GW), F32)
        for j in range(0, S_HPG, 2):
            y_diag = y_diag + jnp.dot(
                jnp.concatenate([w_heads[j], w_heads[j + 1]], axis=1),
                jnp.concatenate([xg_b * hmask[j], xg_b * hmask[j + 1]], axis=0),
                preferred_element_type=F32)
        e_cum = e_ref[pl.ds(e_rows, L), gcols]
        e_end = e_ref[pl.ds(e_rows + L, L), gcols]
        e_tot = e_ref[pl.ds(e_rows + L - SUBLANE, SUBLANE), gcols][SUBLANE - 1:SUBLANE, :]
        y_off = jnp.dot(cg, st_ref[g].astype(BF16), preferred_element_type=F32) * e_cum
        y = y_diag + y_off + p["sd"][:, gcols] * xg
        yz = y * big_ref[rows, Z0 + g * S_GW:Z0 + (g + 1) * S_GW].astype(F32)
        ms = jnp.mean(yz * yz, axis=1, keepdims=True)
        ys_ref[rows, gcols] = (yz * lax.rsqrt(ms + EPS) * p["sng"][:, gcols]).astype(BF16)
        xw = (xg * e_end).astype(BF16)
        s_new = lax.dot_general(bg, xw, (((0,), (0,)), ((), ())), preferred_element_type=F32)
        st_ref[g] = st_ref[g] * e_tot + s_new


_SMALL_KEYS = ("gbias", "alog", "mng", "sd", "sng", "expand")


def _init_kernel(big_ref, gate_ref, gbias_ref, alog_ref, cw_ref, cb_ref, expand_ref,
                 ct_ref, n_ref, m_ref, st_ref):
    L = CHUNK
    act, cum = _gate_block(gate_ref[...] + gbias_ref[...], alog_ref[...], True, _tril3(L))
    ct_ref[...] = jnp.zeros_like(ct_ref)
    n_ref[...] = jnp.zeros_like(n_ref)
    m_ref[...] = jnp.zeros_like(m_ref)
    for h in range(M_HEADS):
        k = big_ref[:, K0 + h * M_DQK:K0 + (h + 1) * M_DQK]
        v = big_ref[:, V0 + h * M_DV:V0 + (h + 1) * M_DV]
        gi, gf = GI0 + h, GF0 + h
        bt_b = _lane_rep(cum, gf)
        _mlstm_state(k, v, bt_b, _lane_rep(act, gi), bt_b[L - 1:L, :], m_ref[h:h + 1, :],
                     ct_ref, n_ref, m_ref, h)

    e_all = _head_expand(act, cum, expand_ref)
    valid = lax.broadcasted_iota(jnp.int32, (L, S_XBC), 0) >= (L - N_META)
    xa = jnp.where(valid, big_ref[:, X0:X0 + S_XBC].astype(F32), 0.0)
    xc = _conv_silu(xa, jnp.zeros((SUBLANE, S_XBC), F32), cw_ref, cb_ref, slice(0, S_XBC))
    for g in range(S_GROUPS):
        gcols = slice(g * S_GW, (g + 1) * S_GW)
        bg = xc[:, XB0 + g * S_STATE:XB0 + (g + 1) * S_STATE].astype(BF16)
        xw = (xc[:, gcols] * e_all[L:2 * L, gcols]).astype(BF16)
        st_ref[g] = lax.dot_general(bg, xw, (((0,), (0,)), ((), ())), preferred_element_type=F32)


_STATE_SHAPES = (
    (M_HEADS, M_DQK, M_DV),
    (SUBLANE, M_DQK),
    (SUBLANE, LANE),
    (S_GROUPS, S_STATE, S_GW),
)


def _init_state(big_m, gate_m, consts):
    return pl.pallas_call(
        _init_kernel,
        grid=(1,),
        in_specs=[_const_spec(big_m.shape), _const_spec(gate_m.shape)] + [_const_spec(a.shape) for a in consts],
        out_specs=[_const_spec(s) for s in _STATE_SHAPES],
        out_shape=[jax.ShapeDtypeStruct(s, F32) for s in _STATE_SHAPES],
        compiler_params=pltpu.CompilerParams(
            dimension_semantics=("arbitrary",), vmem_limit_bytes=VMEM_LIMIT),
        name="init_state",
    )(big_m, gate_m, *consts)


def _mixer_kernel(big_ref, gate_ref, x_ref, *refs):
    ns = len(_SMALL_KEYS)
    p = dict(zip(_SMALL_KEYS, refs[:ns]))
    ct0, n0, m0, st0 = refs[ns:ns + 4]
    mproj_ref, sproj_ref, wout_ref = refs[ns + 4:ns + 7]
    h1_ref = refs[ns + 7]
    ct_ref, n_ref, m_ref, st_ref, hm_ref, ys_ref, mg_ref = refs[ns + 8:ns + 15]
    gs = dict(zip(("rep", "act_t", "cum_t", "expanded"), refs[ns + 15:]))

    @pl.when(pl.program_id(1) == 0)
    def _():
        ct_ref[...] = ct0[...]
        n_ref[...] = n0[...]
        m_ref[...] = m0[...]
        st_ref[...] = st0[...]

    tril3 = _tril3(CHUNK)
    for c in range(CPT):
        rs = slice(c * CHUNK, (c + 1) * CHUNK)
        act, cum = _gate_block(gate_ref[rs, :] + p["gbias"][...], p["alog"][...], False, tril3)
        for h in range(M_HEADS):
            gs["rep"][h, rs, :] = _lane_rep(cum, GF0 + h)
            gs["rep"][M_HEADS + h, rs, :] = _lane_rep(act, GI0 + h)
        for j in range(S_HEADS):
            gs["rep"][2 * M_HEADS + j, rs, :] = _lane_rep(cum, GDT0 + j)
        gs["act_t"][rs, :] = act.T
        gs["cum_t"][rs, :] = cum.T
        gs["expanded"][2 * c * CHUNK:2 * (c + 1) * CHUNK, :] = _head_expand(act, cum, p["expand"])

    lane_g = lax.broadcasted_iota(jnp.int32, (CHUNK, S_GW), 1)
    hmask = [((lane_g >= j * S_HEADDIM) & (lane_g < (j + 1) * S_HEADDIM)).astype(F32).astype(BF16)
             for j in range(S_HPG)]

    def body(c, carry):
        r0 = pl.multiple_of(c * CHUNK, CHUNK)
        _chunk_step(r0, big_ref, gs, p, ct_ref, n_ref, m_ref, st_ref, hm_ref, ys_ref, hmask)
        return carry

    def project(rs):
        for nb in range(D_MODEL // FB):
            cols = slice(nb * FB, (nb + 1) * FB)
            br_a = jnp.dot(hm_ref[rs, :], mproj_ref[:, cols], preferred_element_type=F32)
            br_b = jnp.dot(ys_ref[rs, :], sproj_ref[:, cols], preferred_element_type=F32)
            ga = big_ref[rs, GA0 + nb * FB:GA0 + (nb + 1) * FB].astype(F32)
            gb = big_ref[rs, GB0 + nb * FB:GB0 + (nb + 1) * FB].astype(F32)
            mg_ref[rs, cols] = (ga * br_a + gb * br_b).astype(BF16)
        for nb in range(D_MODEL // FB):
            cols = slice(nb * FB, (nb + 1) * FB)
            h1_ref[rs, cols] = x_ref[rs, cols] + jnp.dot(mg_ref[rs, :], wout_ref[:, cols], preferred_element_type=F32)

    for c in range(CPT // PROJ_CHUNKS):
        for cc in range(PROJ_CHUNKS):
            body(c * PROJ_CHUNKS + cc, 0)
        project(slice(c * PROJ_CHUNKS * CHUNK, (c + 1) * PROJ_CHUNKS * CHUNK))


def _mixer(big, gate, x2, small, init, mproj, sproj, wout, batch):
    n = x2.shape[0]
    tm = MIX_TM
    tiles = n // batch // tm
    row_map = lambda b, j: (b * tiles + j, 0)
    state_shapes = _STATE_SHAPES
    return pl.pallas_call(
        _mixer_kernel,
        grid=(batch, tiles),
        in_specs=[
            pl.BlockSpec((tm, BIG_W), row_map),
            pl.BlockSpec((tm, GATE_W), row_map),
            pl.BlockSpec((tm, D_MODEL), row_map),
        ] + [_const_spec(s.shape) for s in small]
          + [_const_spec(s) for s in state_shapes]
          + [_const_spec((D_MODEL, D_MODEL))] * 3,
        out_specs=pl.BlockSpec((tm, D_MODEL), row_map),
        out_shape=jax.ShapeDtypeStruct((n, D_MODEL), F32),
        scratch_shapes=[pltpu.VMEM(s, F32) for s in state_shapes] + [
            pltpu.VMEM((tm, M_V), BF16),
            pltpu.VMEM((tm, S_INNER), BF16),
            pltpu.VMEM((tm, D_MODEL), BF16),
            pltpu.VMEM((2 * M_HEADS + S_HEADS, tm, LANE), F32),
            pltpu.VMEM((tm, GATE_W), F32),
            pltpu.VMEM((tm, GATE_W), F32),
            pltpu.VMEM((2 * tm, S_INNER), F32),
        ],
        compiler_params=pltpu.CompilerParams(
            dimension_semantics=("arbitrary", "arbitrary"), vmem_limit_bytes=VMEM_LIMIT),
        name="mixer",
    )(big, gate, x2, *small, *init, mproj, sproj, wout)


def _ffn_kernel(h_ref, g2_ref, gf_ref, w1_ref, w2_ref, o_ref, u_ref, hid_ref):
    h = h_ref[...]
    ms = jnp.mean(h * h, axis=-1, keepdims=True)
    u_ref[...] = (h * lax.rsqrt(ms + EPS) * g2_ref[...]).astype(BF16)
    for jb in range(D_FF // FB):
        gate = jnp.dot(u_ref[...], w1_ref[:, jb * FB:(jb + 1) * FB], preferred_element_type=F32)
        up = jnp.dot(u_ref[...], w1_ref[:, D_FF + jb * FB:D_FF + (jb + 1) * FB], preferred_element_type=F32)
        hid_ref[:, jb * FB:(jb + 1) * FB] = (gate * jax.nn.sigmoid(gate) * up).astype(BF16)
    for nb in range(D_MODEL // FB):
        cols = slice(nb * FB, (nb + 1) * FB)
        o_ref[:, cols] = h_ref[:, cols] + jnp.dot(hid_ref[...], w2_ref[:, cols], preferred_element_type=F32)
    h2 = o_ref[...]
    ms2 = jnp.mean(h2 * h2, axis=-1, keepdims=True)
    o_ref[...] = h2 * lax.rsqrt(ms2 + EPS) * gf_ref[...]


def _ffn(h1, g2, gf, w1, w2):
    n = h1.shape[0]
    tm = FFN_TM
    return pl.pallas_call(
        _ffn_kernel,
        grid=(n // tm,),
        in_specs=[
            pl.BlockSpec((tm, D_MODEL), lambda i: (i, 0)),
            _const_spec((1, D_MODEL)),
            _const_spec((1, D_MODEL)),
            _const_spec((D_MODEL, 2 * D_FF)),
            _const_spec((D_FF, D_MODEL)),
        ],
        out_specs=pl.BlockSpec((tm, D_MODEL), lambda i: (i, 0)),
        out_shape=jax.ShapeDtypeStruct((n, D_MODEL), F32),
        scratch_shapes=[pltpu.VMEM((tm, D_MODEL), BF16), pltpu.VMEM((tm, D_FF), BF16)],
        compiler_params=pltpu.CompilerParams(
            dimension_semantics=("arbitrary",), vmem_limit_bytes=VMEM_LIMIT),
        name="ffn",
    )(h1, g2, gf, w1, w2)


def _pad_lanes(row, width):
    return jnp.pad(row, ((0, 0), (0, width - row.shape[1])))


def kernel(x, meta, norm1_g, w_in, m_igate_b, m_fgate_b, m_norm_g, m_proj, s_conv_w, s_conv_b,
           s_dt_bias, s_A_log, s_D, s_norm_g, s_proj, w_out, norm2_g, w_ffn_in, w_ffn_out, norm_f_g):
    bsz, seq, dm = x.shape
    assert dm == D_MODEL and seq % TM == 0 and w_in.shape[0] == 1
    w_all = _pack_w_in(w_in[0].T)

    gbias = _pad_lanes(jnp.concatenate([m_igate_b[0], m_fgate_b[0], s_dt_bias[0]])[None].astype(F32), GATE_W)
    alog = _pad_lanes(jnp.concatenate([jnp.zeros((GDT0,), F32), s_A_log[0].astype(F32)])[None], GATE_W)
    conv_w = jnp.pad(s_conv_w[0].astype(F32), ((0, SUBLANE - S_CONV), (0, 0)))
    conv_b = s_conv_b[0][None].astype(F32)
    head_of_lane = jnp.arange(S_INNER) // S_HEADDIM
    expand1 = (jnp.arange(GATE_W)[:, None] == GDT0 + head_of_lane[None, :]).astype(BF16)
    expand = jnp.concatenate([expand1, expand1], axis=0)
    small = (
        gbias, alog,
        m_norm_g[0].reshape(1, M_V).astype(F32),
        jnp.repeat(s_D[0].astype(F32), S_HEADDIM)[None],
        s_norm_g[0].reshape(1, S_INNER).astype(F32),
        expand,
    )
    g1 = norm1_g[0][None].astype(F32)

    meta_rows = jnp.concatenate([jnp.zeros((CHUNK - N_META, D_MODEL), x.dtype), meta.astype(x.dtype)], axis=0)
    x2 = x.reshape(bsz * seq, dm)
    big, gate, big_m, gate_m = _inproj(x2, g1, w_all, meta_rows, conv_w, conv_b, bsz)
    init = _init_state(big_m, gate_m, (gbias, alog, conv_w, conv_b, expand))
    h1 = _mixer(big, gate, x2, small, init, m_proj[0].astype(BF16), s_proj[0].astype(BF16),
                w_out[0].astype(BF16), bsz)
    out = _ffn(h1, norm2_g[0][None].astype(F32), norm_f_g[None].astype(F32),
               w_ffn_in[0].astype(BF16), w_ffn_out[0].astype(BF16))
    return out.reshape(bsz, seq, dm)
```

```python
import functools
import math

import jax
import jax.numpy as jnp
from jax import lax
from jax.experimental import pallas as pl
from jax.experimental.pallas import tpu as pltpu

F32 = jnp.float32
BF16 = jnp.bfloat16
LOG2E = math.log2(math.e)

D_MODEL = 1024
N_META = 16
CHUNK = 128
EPS = 1e-6

M_HEADS = 4
M_DQK = 128
M_DV = 256
M_QK = M_HEADS * M_DQK
M_V = M_HEADS * M_DV
GATE_CAP = 15.0

S_HEADDIM = 64
S_INNER = D_MODEL
S_HEADS = S_INNER // S_HEADDIM
S_GROUPS = 4
S_HPG = S_HEADS // S_GROUPS
S_STATE = 128
S_CONV = 4
S_GW = S_HPG * S_HEADDIM
S_XBC = S_INNER + 2 * S_GROUPS * S_STATE
XB0 = S_INNER
XC0 = S_INNER + S_GROUPS * S_STATE

D_FF = 2816

Q0 = 0
K0 = Q0 + M_QK
V0 = K0 + M_QK
O0 = V0 + M_V
Z0 = O0 + M_V
X0 = Z0 + S_INNER
GA0 = X0 + S_XBC
GB0 = GA0 + D_MODEL
BIG_W = GB0 + D_MODEL
GATE_W = 128
GI0, GF0, GDT0, GEND = 0, M_HEADS, 2 * M_HEADS, 2 * M_HEADS + S_HEADS

LANE = 128
SUBLANE = 8
TM = 512
MIX_TM = 512
FFN_TM = 1024
CPT = MIX_TM // CHUNK
PROJ_CHUNKS = 2
NB = 512
FB = 256
VMEM_LIMIT = 56 * 1024 * 1024


def _softplus(x):
    return jnp.maximum(x, 0.0) + jnp.log1p(jnp.exp(-jnp.abs(x)))


def _const_spec(shape):
    nd = len(shape)
    return pl.BlockSpec(shape, lambda *_: (0,) * nd, pipeline_mode=pl.Buffered(1))


def _conv_silu(xa, hist, cw_ref, cb_ref, cols):
    row8 = lax.broadcasted_iota(jnp.int32, (SUBLANE, xa.shape[1]), 0)
    acc = cb_ref[:, cols] + xa * cw_ref[S_CONV - 1:S_CONV, cols]
    for sh in range(1, S_CONV):
        rolled = pltpu.roll(xa, sh, axis=0)
        top = jnp.where(row8 < sh, pltpu.roll(hist, sh, axis=0), rolled[0:SUBLANE])
        shifted = jnp.concatenate([top, rolled[SUBLANE:]], axis=0)
        acc = acc + shifted * cw_ref[S_CONV - 1 - sh:S_CONV - sh, cols]
    return acc * jax.nn.sigmoid(acc)


_IN_SIZES = (M_QK, M_QK, M_V, M_V, M_HEADS, M_HEADS, S_INNER, S_XBC, S_HEADS, D_MODEL, D_MODEL)
_IN_OFFS = tuple(sum(_IN_SIZES[:i]) for i in range(len(_IN_SIZES) + 1))
IN_WIDTH = _IN_OFFS[-1]
_PACK_SEGMENTS = (
    (_IN_OFFS[0], _IN_OFFS[4], Q0),
    (_IN_OFFS[6], _IN_OFFS[7], Z0),
    (_IN_OFFS[7], _IN_OFFS[8], X0),
    (_IN_OFFS[9], _IN_OFFS[10], GA0),
    (_IN_OFFS[10], _IN_OFFS[11], GB0),
)
PACK_ROWS = 512


def _pack_kernel(wt_ref, o_ref):
    for s0, s1, d0 in _PACK_SEGMENTS:
        for r in range(s0, s1, PACK_ROWS):
            o_ref[:, d0 + r - s0:d0 + r - s0 + PACK_ROWS] = wt_ref[r:r + PACK_ROWS, :].T.astype(BF16)
    kb = wt_ref.shape[1]
    gates = jnp.concatenate(
        [wt_ref[_IN_OFFS[4]:_IN_OFFS[6], :], wt_ref[_IN_OFFS[8]:_IN_OFFS[9], :],
         jnp.zeros((GATE_W - GEND, kb), F32)], axis=0)
    o_ref[:, BIG_W:BIG_W + GATE_W] = gates.T.astype(BF16)


def _pack_w_in(wt):
    kb = LANE
    return pl.pallas_call(
        _pack_kernel,
        grid=(D_MODEL // kb,),
        in_specs=[pl.BlockSpec((IN_WIDTH, kb), lambda i: (0, i))],
        out_specs=pl.BlockSpec((kb, BIG_W + GATE_W), lambda i: (i, 0)),
        out_shape=jax.ShapeDtypeStruct((D_MODEL, BIG_W + GATE_W), BF16),
        compiler_params=pltpu.CompilerParams(
            dimension_semantics=("arbitrary",), vmem_limit_bytes=VMEM_LIMIT),
        name="pack_w_in",
    )(wt)


def _inproj_kernel(x_ref, g_ref, w_ref, meta_ref, cw_ref, cb_ref,
                   big_ref, gate_ref, bigm_ref, gatem_ref, u_ref, cs_ref, xh0_ref):
    tm = x_ref.shape[0]

    def normed(rows):
        ms = jnp.mean(rows * rows, axis=-1, keepdims=True)
        return (rows * lax.rsqrt(ms + EPS) * g_ref[...]).astype(BF16)

    @pl.when((pl.program_id(0) == 0) & (pl.program_id(1) == 0))
    def _():
        um = normed(meta_ref[...])
        bigm_ref[...] = jnp.zeros_like(bigm_ref)
        for c0 in list(range(K0, O0, NB)) + list(range(X0, GA0, NB)):
            acc = jnp.dot(um, w_ref[:, c0:c0 + NB], preferred_element_type=F32)
            bigm_ref[:, c0:c0 + NB] = acc.astype(BF16)
            if c0 >= X0:
                xh0_ref[:, c0 - X0:c0 - X0 + NB] = acc[CHUNK - SUBLANE:CHUNK, :]
        gatem_ref[...] = jnp.dot(um, w_ref[:, BIG_W:BIG_W + GATE_W], preferred_element_type=F32)

    @pl.when(pl.program_id(1) == 0)
    def _():
        for s in range(S_XBC // LANE):
            cs_ref[s, 0:SUBLANE, :] = xh0_ref[:, s * LANE:(s + 1) * LANE]

    u_ref[...] = normed(x_ref[...])
    for n in range(BIG_W // NB):
        c0 = n * NB
        acc = jnp.dot(u_ref[...], w_ref[:, c0:c0 + NB], preferred_element_type=F32)
        if c0 < K0:
            acc = acc * (M_DQK ** -0.5)
        elif O0 <= c0 < Z0 or c0 >= GA0:
            acc = jax.nn.sigmoid(acc)
        elif Z0 <= c0 < X0:
            acc = acc * jax.nn.sigmoid(acc)
        elif X0 <= c0 < GA0:
            for s in range(NB // LANE):
                slab = (c0 - X0) // LANE + s
                cols = slice((c0 - X0) + s * LANE, (c0 - X0) + (s + 1) * LANE)
                cs_ref[slab, SUBLANE:SUBLANE + tm, :] = acc[:, s * LANE:(s + 1) * LANE]
                half = 0.5 * cb_ref[:, cols]
                for tap in range(S_CONV):
                    back = S_CONV - 1 - tap
                    half = half + (cs_ref[slab, pl.ds(SUBLANE - back, tm, stride=1), :]
                                   * (0.5 * cw_ref[tap:tap + 1, cols]))
                cs_ref[slab, 0:SUBLANE, :] = cs_ref[slab, tm:tm + SUBLANE, :]
                big_ref[:, c0 + s * LANE:c0 + (s + 1) * LANE] = (half + half * jnp.tanh(half)).astype(BF16)
            continue
        big_ref[:, c0:c0 + NB] = acc.astype(BF16)
    gate_ref[...] = jnp.dot(u_ref[...], w_ref[:, BIG_W:BIG_W + GATE_W], preferred_element_type=F32)


def _inproj(rows, g, w_all, meta_rows, conv_w, conv_b, batch):
    n = rows.shape[0]
    tm = TM
    tiles = n // batch // tm
    row_map = lambda b, j: (b * tiles + j, 0)
    consts = (g, w_all, meta_rows, conv_w, conv_b)
    return pl.pallas_call(
        _inproj_kernel,
        grid=(batch, tiles),
        in_specs=[pl.BlockSpec((tm, D_MODEL), row_map)] + [_const_spec(a.shape) for a in consts],
        out_specs=[
            pl.BlockSpec((tm, BIG_W), row_map),
            pl.BlockSpec((tm, GATE_W), row_map),
            _const_spec((CHUNK, BIG_W)),
            _const_spec((CHUNK, GATE_W)),
        ],
        out_shape=[
            jax.ShapeDtypeStruct((n, BIG_W), BF16),
            jax.ShapeDtypeStruct((n, GATE_W), F32),
            jax.ShapeDtypeStruct((CHUNK, BIG_W), BF16),
            jax.ShapeDtypeStruct((CHUNK, GATE_W), F32),
        ],
        scratch_shapes=[
            pltpu.VMEM((tm, D_MODEL), BF16),
            pltpu.VMEM((S_XBC // LANE, SUBLANE + tm, LANE), F32),
            pltpu.VMEM((SUBLANE, S_XBC), F32),
        ],
        compiler_params=pltpu.CompilerParams(
            dimension_semantics=("arbitrary", "arbitrary"), vmem_limit_bytes=VMEM_LIMIT),
        name="inproj",
    )(rows, *consts)


def _gate_block(pre, alog_row, masked, tril3):
    L = pre.shape[0]
    lane = lax.broadcasted_iota(jnp.int32, (L, GATE_W), 1)
    sc = GATE_CAP * jnp.tanh(pre / GATE_CAP)
    i_log = sc * LOG2E
    f_log = -_softplus(-sc) * LOG2E
    dt = _softplus(pre)
    if masked:
        valid = lax.broadcasted_iota(jnp.int32, (L, GATE_W), 0) >= (L - N_META)
        i_log = jnp.where(valid, i_log, -jnp.inf)
        f_log = jnp.where(valid, f_log, 0.0)
        dt = jnp.where(valid, dt, 0.0)
    act = jnp.where(lane < GF0, i_log, jnp.where(lane < GDT0, f_log, jnp.where(lane < GEND, dt, 0.0)))
    a_row = -jnp.exp(alog_row) * LOG2E
    cs = jnp.where(lane < GF0, 0.0, jnp.where(lane < GDT0, act, jnp.where(lane < GEND, act * a_row, 0.0)))
    hi = cs.astype(BF16)
    rem = cs - hi.astype(F32)
    mid = rem.astype(BF16)
    lo = (rem - mid.astype(F32)).astype(BF16)
    cum = jnp.dot(tril3, jnp.concatenate([hi, mid, lo], axis=0), preferred_element_type=F32)
    return act, cum


def _tril3(L):
    assert L & (L - 1) == 0
    row_i = lax.broadcasted_iota(jnp.int32, (L, 3 * L), 0)
    col_i = lax.broadcasted_iota(jnp.int32, (L, 3 * L), 1)
    return (jnp.bitwise_and(col_i, L - 1) <= row_i).astype(F32).astype(BF16)


def _lane_rep(x, col):
    return jnp.broadcast_to(x[:, col:col + 1], (x.shape[0], LANE))


def _mlstm_state(k, v, bt_b, it_b, b_tot, m_prev, ct_ref, n_ref, m_ref, h):
    w_end = b_tot - bt_b + it_b
    m_loc = jnp.max(w_end, axis=0, keepdims=True)
    wgt = jnp.exp2(w_end - m_loc)
    kw = k.astype(F32) * wgt
    s_loc = lax.dot_general(kw.astype(BF16), v, (((0,), (0,)), ((), ())), preferred_element_type=F32)
    n_loc = jnp.sum(kw, axis=0, keepdims=True)
    m_new = jnp.maximum(b_tot + m_prev, m_loc)
    a_dec = jnp.exp2(b_tot + m_prev - m_new)
    s_dec = jnp.exp2(m_loc - m_new)
    ct_ref[h] = (jnp.concatenate([a_dec, a_dec], axis=1) * ct_ref[h]
                 + jnp.concatenate([s_dec, s_dec], axis=1) * s_loc)
    n_ref[h:h + 1, :] = a_dec * n_ref[h:h + 1, :] + s_dec * n_loc
    m_ref[h:h + 1, :] = m_new


def _head_expand(act, cum, expand_ref):
    L = act.shape[0]
    lane = lax.broadcasted_iota(jnp.int32, (L, GATE_W), 1)
    is_dt = (lane >= GDT0) & (lane < GEND)
    tot = cum[L - 1:L, :]
    e_cum = jnp.where(is_dt, jnp.exp2(cum), 0.0)
    e_end = jnp.where(is_dt, jnp.exp2(tot - cum) * act, 0.0)

    def split(x):
        hi = x.astype(BF16)
        lo = (x - hi.astype(F32)).astype(BF16)
        return jnp.concatenate([hi, lo], axis=1)

    lhs = jnp.concatenate([split(e_cum), split(e_end)], axis=0)
    return jnp.dot(lhs, expand_ref[...], preferred_element_type=F32)


def _chunk_step(r0, big_ref, gs, p, ct_ref, n_ref, m_ref, st_ref, hm_ref, ys_ref, hmask):
    L = CHUNK
    rows = pl.ds(r0, L)
    row_i = lax.broadcasted_iota(jnp.int32, (L, L), 0)
    col_i = lax.broadcasted_iota(jnp.int32, (L, L), 1)
    causal = col_i <= row_i
    neg_inf = -jnp.inf

    rep_ref = gs["rep"]
    act_t = gs["act_t"][rows, :]
    cum_t = gs["cum_t"][rows, :]
    e_rows = pl.multiple_of(2 * r0, 2 * L)

    zero_blk = jnp.zeros((L, LANE), BF16)

    def paired_nt(lhs_pair, rhs0, rhs1):
        rhs = jnp.concatenate([jnp.concatenate([rhs0, zero_blk], axis=1),
                               jnp.concatenate([zero_blk, rhs1], axis=1)], axis=0)
        out = lax.dot_general(lhs_pair, rhs, (((1,), (1,)), ((), ())), preferred_element_type=F32)
        return out[:, :LANE], out[:, LANE:]

    qk_heads = []
    for pr in range(M_HEADS // 2):
        qk_heads += paired_nt(
            big_ref[rows, Q0 + 2 * pr * M_DQK:Q0 + 2 * (pr + 1) * M_DQK],
            big_ref[rows, K0 + 2 * pr * M_DQK:K0 + (2 * pr + 1) * M_DQK],
            big_ref[rows, K0 + (2 * pr + 1) * M_DQK:K0 + 2 * (pr + 1) * M_DQK])

    def mlstm_head(h):
        q = big_ref[rows, Q0 + h * M_DQK:Q0 + (h + 1) * M_DQK]
        k = big_ref[rows, K0 + h * M_DQK:K0 + (h + 1) * M_DQK]
        v = big_ref[rows, V0 + h * M_DV:V0 + (h + 1) * M_DV]
        gi, gf = GI0 + h, GF0 + h
        bt_b = rep_ref[h, rows, :]
        it_b = rep_ref[M_HEADS + h, rows, :]
        bt_r = cum_t[gf:gf + 1, :]
        it_r = act_t[gi:gi + 1, :]
        b_tot = bt_b[L - 1:L, :]
        m_prev = m_ref[h:h + 1, :]
        qk = qk_heads[h]
        d_log = jnp.where(causal, bt_b - bt_r + it_r, neg_inf)
        mx = jnp.max(d_log, axis=1, keepdims=True)
        inter_log = bt_b + m_prev
        m_t = jnp.maximum(inter_log, mx)
        w_ts = jnp.exp2(d_log - m_t) * qk
        inter = jnp.exp2(inter_log - m_t)
        q_f = q.astype(F32)
        lhs = jnp.concatenate([w_ts.astype(BF16), (q_f * inter).astype(BF16)], axis=1)
        rhs = jnp.concatenate([v, ct_ref[h].astype(BF16)], axis=0)
        num = jnp.dot(lhs, rhs, preferred_element_type=F32)
        q_n = jnp.sum(q_f * n_ref[h:h + 1, :], axis=1, keepdims=True)
        den = jnp.sum(w_ts, axis=1, keepdims=True) + inter * q_n
        denom = jnp.maximum(jnp.abs(den), jnp.exp2(-m_t))
        ms = jnp.mean(num * num, axis=1, keepdims=True)
        rs = lax.rsqrt(ms + EPS * (denom * denom))
        y = num * jnp.concatenate([rs, rs], axis=1) * p["mng"][:, h * M_DV:(h + 1) * M_DV]
        o_gate = big_ref[rows, O0 + h * M_DV:O0 + (h + 1) * M_DV]
        hm_ref[rows, h * M_DV:(h + 1) * M_DV] = y.astype(BF16) * o_gate
        _mlstm_state(k, v, bt_b, it_b, b_tot, m_prev, ct_ref, n_ref, m_ref, h)

    e_ref = gs["expanded"]
    cb_groups = []
    for pr in range(S_GROUPS // 2):
        cb_groups += paired_nt(
            big_ref[rows, X0 + XC0 + 2 * pr * S_STATE:X0 + XC0 + 2 * (pr + 1) * S_STATE],
            big_ref[rows, X0 + XB0 + 2 * pr * S_STATE:X0 + XB0 + (2 * pr + 1) * S_STATE],
            big_ref[rows, X0 + XB0 + (2 * pr + 1) * S_STATE:X0 + XB0 + 2 * (pr + 1) * S_STATE])

    def ssd_group(g):
        gcols = slice(g * S_GW, (g + 1) * S_GW)
        bg = big_ref[rows, X0 + XB0 + g * S_STATE:X0 + XB0 + (g + 1) * S_STATE]
        cg = big_ref[rows, X0 + XC0 + g * S_STATE:X0 + XC0 + (g + 1) * S_STATE]
        xg_b = big_ref[rows, X0 + g * S_GW:X0 + (g + 1) * S_GW]
        xg = xg_b.astype(F32)
        cbm = cb_groups[g]
        w_heads = []
        for j in range(S_HPG):
            gk = GDT0 + g * S_HPG + j
            ca_b = rep_ref[2 * M_HEADS + g * S_HPG + j, rows, :]
            ca_r = cum_t[gk:gk + 1, :]
            dt_r = act_t[gk:gk + 1, :]
            dec = jnp.exp2(jnp.where(causal, ca_b - ca_r, neg_inf))
            w_heads.append((cbm * dec * dt_r).astype(BF16))
        y_diag = jnp.zeros((L, S_GW), F32)
        for j in range(0, S_HPG, 2):
            y_diag = y_diag + jnp.dot(
                jnp.concatenate([w_heads[j], w_heads[j + 1]], axis=1),
                jnp.concatenate([xg_b * hmask[j], xg_b * hmask[j + 1]], axis=0),
                preferred_element_type=F32)
        e_cum = e_ref[pl.ds(e_rows, L), gcols]
        e_end = e_ref[pl.ds(e_rows + L, L), gcols]
        e_tot = e_ref[pl.ds(e_rows + L - SUBLANE, SUBLANE), gcols][SUBLANE - 1:SUBLANE, :]
        y_off = jnp.dot(cg, st_ref[g].astype(BF16), preferred_element_type=F32) * e_cum
        y = y_diag + y_off + p["sd"][:, gcols] * xg
        yz = y * big_ref[rows, Z0 + g * S_GW:Z0 + (g + 1) * S_GW].astype(F32)
        ms = jnp.mean(yz * yz, axis=1, keepdims=True)
        ys_ref[rows, gcols] = (yz * lax.rsqrt(ms + EPS) * p["sng"][:, gcols]).astype(BF16)
        xw = (xg * e_end).astype(BF16)
        s_new = lax.dot_general(bg, xw, (((0,), (0,)), ((), ())), preferred_element_type=F32)
        st_ref[g] = st_ref[g] * e_tot + s_new

    assert M_HEADS == S_GROUPS
    for i in range(M_HEADS):
        mlstm_head(i)
        ssd_group(i)


_SMALL_KEYS = ("gbias", "alog", "mng", "sd", "sng", "expand")


def _init_kernel(big_ref, gate_ref, gbias_ref, alog_ref, cw_ref, cb_ref, expand_ref,
                 ct_ref, n_ref, m_ref, st_ref):
    L = CHUNK
    act, cum = _gate_block(gate_ref[...] + gbias_ref[...], alog_ref[...], True, _tril3(L))
    ct_ref[...] = jnp.zeros_like(ct_ref)
    n_ref[...] = jnp.zeros_like(n_ref)
    m_ref[...] = jnp.zeros_like(m_ref)
    for h in range(M_HEADS):
        k = big_ref[:, K0 + h * M_DQK:K0 + (h + 1) * M_DQK]
        v = big_ref[:, V0 + h * M_DV:V0 + (h + 1) * M_DV]
        gi, gf = GI0 + h, GF0 + h
        bt_b = _lane_rep(cum, gf)
        _mlstm_state(k, v, bt_b, _lane_rep(act, gi), bt_b[L - 1:L, :], m_ref[h:h + 1, :],
                     ct_ref, n_ref, m_ref, h)

    e_all = _head_expand(act, cum, expand_ref)
    valid = lax.broadcasted_iota(jnp.int32, (L, S_XBC), 0) >= (L - N_META)
    xa = jnp.where(valid, big_ref[:, X0:X0 + S_XBC].astype(F32), 0.0)
    xc = _conv_silu(xa, jnp.zeros((SUBLANE, S_XBC), F32), cw_ref, cb_ref, slice(0, S_XBC))
    for g in range(S_GROUPS):
        gcols = slice(g * S_GW, (g + 1) * S_GW)
        bg = xc[:, XB0 + g * S_STATE:XB0 + (g + 1) * S_STATE].astype(BF16)
        xw = (xc[:, gcols] * e_all[L:2 * L, gcols]).astype(BF16)
        st_ref[g] = lax.dot_general(bg, xw, (((0,), (0,)), ((), ())), preferred_element_type=F32)


_STATE_SHAPES = (
    (M_HEADS, M_DQK, M_DV),
    (SUBLANE, M_DQK),
    (SUBLANE, LANE),
    (S_GROUPS, S_STATE, S_GW),
)


def _init_state(big_m, gate_m, consts):
    return pl.pallas_call(
        _init_kernel,
        grid=(1,),
        in_specs=[_const_spec(big_m.shape), _const_spec(gate_m.shape)] + [_const_spec(a.shape) for a in consts],
        out_specs=[_const_spec(s) for s in _STATE_SHAPES],
        out_shape=[jax.ShapeDtypeStruct(s, F32) for s in _STATE_SHAPES],
        compiler_params=pltpu.CompilerParams(
            dimension_semantics=("arbitrary",), vmem_limit_bytes=VMEM_LIMIT),
        name="init_state",
    )(big_m, gate_m, *consts)


def _mixer_kernel(big_ref, gate_ref, x_ref, *refs):
    ns = len(_SMALL_KEYS)
    p = dict(zip(_SMALL_KEYS, refs[:ns]))
    ct0, n0, m0, st0 = refs[ns:ns + 4]
    mproj_ref, sproj_ref, wout_ref = refs[ns + 4:ns + 7]
    h1_ref = refs[ns + 7]
    ct_ref, n_ref, m_ref, st_ref, hm_ref, ys_ref, mg_ref = refs[ns + 8:ns + 15]
    gs = dict(zip(("rep", "act_t", "cum_t", "expanded"), refs[ns + 15:]))

    @pl.when(pl.program_id(1) == 0)
    def _():
        ct_ref[...] = ct0[...]
        n_ref[...] = n0[...]
        m_ref[...] = m0[...]
        st_ref[...] = st0[...]

    tril3 = _tril3(CHUNK)
    for c in range(CPT):
        rs = slice(c * CHUNK, (c + 1) * CHUNK)
        act, cum = _gate_block(gate_ref[rs, :] + p["gbias"][...], p["alog"][...], False, tril3)
        for h in range(M_HEADS):
            gs["rep"][h, rs, :] = _lane_rep(cum, GF0 + h)
            gs["rep"][M_HEADS + h, rs, :] = _lane_rep(act, GI0 + h)
        for j in range(S_HEADS):
            gs["rep"][2 * M_HEADS + j, rs, :] = _lane_rep(cum, GDT0 + j)
        gs["act_t"][rs, :] = act.T
        gs["cum_t"][rs, :] = cum.T
        gs["expanded"][2 * c * CHUNK:2 * (c + 1) * CHUNK, :] = _head_expand(act, cum, p["expand"])

    lane_g = lax.broadcasted_iota(jnp.int32, (CHUNK, S_GW), 1)
    hmask = [((lane_g >= j * S_HEADDIM) & (lane_g < (j + 1) * S_HEADDIM)).astype(F32).astype(BF16)
             for j in range(S_HPG)]

    def body(c, carry):
        r0 = pl.multiple_of(c * CHUNK, CHUNK)
        _chunk_step(r0, big_ref, gs, p, ct_ref, n_ref, m_ref, st_ref, hm_ref, ys_ref, hmask)
        return carry

    def merge_branches(rs):
        for nb in range(D_MODEL // FB):
            cols = slice(nb * FB, (nb + 1) * FB)
            br_a = jnp.dot(hm_ref[rs, :], mproj_ref[:, cols], preferred_element_type=F32)
            br_b = jnp.dot(ys_ref[rs, :], sproj_ref[:, cols], preferred_element_type=F32)
            ga = big_ref[rs, GA0 + nb * FB:GA0 + (nb + 1) * FB].astype(F32)
            gb = big_ref[rs, GB0 + nb * FB:GB0 + (nb + 1) * FB].astype(F32)
            mg_ref[rs, cols] = (ga * br_a + gb * br_b).astype(BF16)

    def project_out(rs):
        for nb in range(D_MODEL // FB):
            cols = slice(nb * FB, (nb + 1) * FB)
            h1_ref[rs, cols] = x_ref[rs, cols] + jnp.dot(mg_ref[rs, :], wout_ref[:, cols], preferred_element_type=F32)

    groups = [slice(c * PROJ_CHUNKS * CHUNK, (c + 1) * PROJ_CHUNKS * CHUNK) for c in range(CPT // PROJ_CHUNKS)]
    pending = []
    for c in range(CPT):
        body(c, 0)
        if pending:
            pending.pop(0)()
        if (c + 1) % PROJ_CHUNKS == 0:
            rs = groups[c // PROJ_CHUNKS]
            merge_branches(rs)
            pending.append(functools.partial(project_out, rs))
    for fn in pending:
        fn()


def _mixer(big, gate, x2, small, init, mproj, sproj, wout, batch):
    n = x2.shape[0]
    tm = MIX_TM
    tiles = n // batch // tm
    row_map = lambda b, j: (b * tiles + j, 0)
    state_shapes = _STATE_SHAPES
    return pl.pallas_call(
        _mixer_kernel,
        grid=(batch, tiles),
        in_specs=[
            pl.BlockSpec((tm, BIG_W), row_map),
            pl.BlockSpec((tm, GATE_W), row_map),
            pl.BlockSpec((tm, D_MODEL), row_map),
        ] + [_const_spec(s.shape) for s in small]
          + [_const_spec(s) for s in state_shapes]
          + [_const_spec((D_MODEL, D_MODEL))] * 3,
        out_specs=pl.BlockSpec((tm, D_MODEL), row_map),
        out_shape=jax.ShapeDtypeStruct((n, D_MODEL), F32),
        scratch_shapes=[pltpu.VMEM(s, F32) for s in state_shapes] + [
            pltpu.VMEM((tm, M_V), BF16),
            pltpu.VMEM((tm, S_INNER), BF16),
            pltpu.VMEM((tm, D_MODEL), BF16),
            pltpu.VMEM((2 * M_HEADS + S_HEADS, tm, LANE), F32),
            pltpu.VMEM((tm, GATE_W), F32),
            pltpu.VMEM((tm, GATE_W), F32),
            pltpu.VMEM((2 * tm, S_INNER), F32),
        ],
        compiler_params=pltpu.CompilerParams(
            dimension_semantics=("arbitrary", "arbitrary"), vmem_limit_bytes=VMEM_LIMIT),
        name="mixer",
    )(big, gate, x2, *small, *init, mproj, sproj, wout)


def _ffn_kernel(h_ref, g2_ref, gf_ref, w1_ref, w2_ref, o_ref, u_ref, hid_ref):
    h = h_ref[...]
    ms = jnp.mean(h * h, axis=-1, keepdims=True)
    u_ref[...] = (h * lax.rsqrt(ms + EPS) * g2_ref[...]).astype(BF16)
    for jb in range(D_FF // FB):
        gate = jnp.dot(u_ref[...], w1_ref[:, jb * FB:(jb + 1) * FB], preferred_element_type=F32)
        up = jnp.dot(u_ref[...], w1_ref[:, D_FF + jb * FB:D_FF + (jb + 1) * FB], preferred_element_type=F32)
        hid_ref[:, jb * FB:(jb + 1) * FB] = (gate * jax.nn.sigmoid(gate) * up).astype(BF16)
    for nb in range(D_MODEL // FB):
        cols = slice(nb * FB, (nb + 1) * FB)
        o_ref[:, cols] = h_ref[:, cols] + jnp.dot(hid_ref[...], w2_ref[:, cols], preferred_element_type=F32)
    h2 = o_ref[...]
    ms2 = jnp.mean(h2 * h2, axis=-1, keepdims=True)
    o_ref[...] = h2 * lax.rsqrt(ms2 + EPS) * gf_ref[...]


def _ffn(h1, g2, gf, w1, w2):
    n = h1.shape[0]
    tm = FFN_TM
    return pl.pallas_call(
        _ffn_kernel,
        grid=(n // tm,),
        in_specs=[
            pl.BlockSpec((tm, D_MODEL), lambda i: (i, 0)),
            _const_spec((1, D_MODEL)),
            _const_spec((1, D_MODEL)),
            _const_spec((D_MODEL, 2 * D_FF)),
            _const_spec((D_FF, D_MODEL)),
        ],
        out_specs=pl.BlockSpec((tm, D_MODEL), lambda i: (i, 0)),
        out_shape=jax.ShapeDtypeStruct((n, D_MODEL), F32),
        scratch_shapes=[pltpu.VMEM((tm, D_MODEL), BF16), pltpu.VMEM((tm, D_FF), BF16)],
        compiler_params=pltpu.CompilerParams(
            dimension_semantics=("arbitrary",), vmem_limit_bytes=VMEM_LIMIT),
        name="ffn",
    )(h1, g2, gf, w1, w2)


def _pad_lanes(row, width):
    return jnp.pad(row, ((0, 0), (0, width - row.shape[1])))


def kernel(x, meta, norm1_g, w_in, m_igate_b, m_fgate_b, m_norm_g, m_proj, s_conv_w, s_conv_b,
           s_dt_bias, s_A_log, s_D, s_norm_g, s_proj, w_out, norm2_g, w_ffn_in, w_ffn_out, norm_f_g):
    bsz, seq, dm = x.shape
    assert dm == D_MODEL and seq % TM == 0 and w_in.shape[0] == 1
    w_all = _pack_w_in(w_in[0].T)

    gbias = _pad_lanes(jnp.concatenate([m_igate_b[0], m_fgate_b[0], s_dt_bias[0]])[None].astype(F32), GATE_W)
    alog = _pad_lanes(jnp.concatenate([jnp.zeros((GDT0,), F32), s_A_log[0].astype(F32)])[None], GATE_W)
    conv_w = jnp.pad(s_conv_w[0].astype(F32), ((0, SUBLANE - S_CONV), (0, 0)))
    conv_b = s_conv_b[0][None].astype(F32)
    head_of_lane = jnp.arange(S_INNER) // S_HEADDIM
    expand1 = (jnp.arange(GATE_W)[:, None] == GDT0 + head_of_lane[None, :]).astype(BF16)
    expand = jnp.concatenate([expand1, expand1], axis=0)
    small = (
        gbias, alog,
        m_norm_g[0].reshape(1, M_V).astype(F32),
        jnp.repeat(s_D[0].astype(F32), S_HEADDIM)[None],
        s_norm_g[0].reshape(1, S_INNER).astype(F32),
        expand,
    )
    g1 = norm1_g[0][None].astype(F32)

    meta_rows = jnp.concatenate([jnp.zeros((CHUNK - N_META, D_MODEL), x.dtype), meta.astype(x.dtype)], axis=0)
    x2 = x.reshape(bsz * seq, dm)
    big, gate, big_m, gate_m = _inproj(x2, g1, w_all, meta_rows, conv_w, conv_b, bsz)
    init = _init_state(big_m, gate_m, (gbias, alog, conv_w, conv_b, expand))
    h1 = _mixer(big, gate, x2, small, init, m_proj[0].astype(BF16), s_proj[0].astype(BF16),
                w_out[0].astype(BF16), bsz)
    out = _ffn(h1, norm2_g[0][None].astype(F32), norm_f_g[None].astype(F32),
               w_ffn_in[0].astype(BF16), w_ffn_out[0].astype(BF16))
    return out.reshape(bsz, seq, dm)
```

```python
import functools
import math

import jax
import jax.numpy as jnp
from jax import lax
from jax.experimental import pallas as pl
from jax.experimental.pallas import tpu as pltpu

F32 = jnp.float32
BF16 = jnp.bfloat16
LOG2E = math.log2(math.e)

D_MODEL = 1024
N_META = 16
CHUNK = 128
EPS = 1e-6

M_HEADS = 4
M_DQK = 128
M_DV = 256
M_QK = M_HEADS * M_DQK
M_V = M_HEADS * M_DV
GATE_CAP = 15.0

S_HEADDIM = 64
S_INNER = D_MODEL
S_HEADS = S_INNER // S_HEADDIM
S_GROUPS = 4
S_HPG = S_HEADS // S_GROUPS
S_STATE = 128
S_CONV = 4
S_GW = S_HPG * S_HEADDIM
S_XBC = S_INNER + 2 * S_GROUPS * S_STATE
XB0 = S_INNER
XC0 = S_INNER + S_GROUPS * S_STATE

D_FF = 2816

Q0 = 0
K0 = Q0 + M_QK
V0 = K0 + M_QK
O0 = V0 + M_V
Z0 = O0 + M_V
X0 = Z0 + S_INNER
GA0 = X0 + S_XBC
GB0 = GA0 + D_MODEL
BIG_W = GB0 + D_MODEL
GATE_W = 128
GI0, GF0, GDT0, GEND = 0, M_HEADS, 2 * M_HEADS, 2 * M_HEADS + S_HEADS

LANE = 128
SUBLANE = 8
TM = 512
MIX_TM = 512
FFN_TM = 1024
CPT = MIX_TM // CHUNK
PROJ_CHUNKS = 2
PRO_AHEAD = 2
NB = 512
FB = 256
VMEM_LIMIT = 56 * 1024 * 1024


def _softplus(x):
    return jnp.maximum(x, 0.0) + jnp.log1p(jnp.exp(-jnp.abs(x)))


def _const_spec(shape):
    nd = len(shape)
    return pl.BlockSpec(shape, lambda *_: (0,) * nd, pipeline_mode=pl.Buffered(1))


def _conv_silu(xa, hist, cw_ref, cb_ref, cols):
    row8 = lax.broadcasted_iota(jnp.int32, (SUBLANE, xa.shape[1]), 0)
    acc = cb_ref[:, cols] + xa * cw_ref[S_CONV - 1:S_CONV, cols]
    for sh in range(1, S_CONV):
        rolled = pltpu.roll(xa, sh, axis=0)
        top = jnp.where(row8 < sh, pltpu.roll(hist, sh, axis=0), rolled[0:SUBLANE])
        shifted = jnp.concatenate([top, rolled[SUBLANE:]], axis=0)
        acc = acc + shifted * cw_ref[S_CONV - 1 - sh:S_CONV - sh, cols]
    return acc * jax.nn.sigmoid(acc)


_IN_SIZES = (M_QK, M_QK, M_V, M_V, M_HEADS, M_HEADS, S_INNER, S_XBC, S_HEADS, D_MODEL, D_MODEL)
_IN_OFFS = tuple(sum(_IN_SIZES[:i]) for i in range(len(_IN_SIZES) + 1))
IN_WIDTH = _IN_OFFS[-1]
_PACK_SEGMENTS = (
    (_IN_OFFS[0], _IN_OFFS[4], Q0),
    (_IN_OFFS[6], _IN_OFFS[7], Z0),
    (_IN_OFFS[7], _IN_OFFS[8], X0),
    (_IN_OFFS[9], _IN_OFFS[10], GA0),
    (_IN_OFFS[10], _IN_OFFS[11], GB0),
)
PACK_ROWS = 512


def _pack_kernel(wt_ref, o_ref):
    for s0, s1, d0 in _PACK_SEGMENTS:
        for r in range(s0, s1, PACK_ROWS):
            o_ref[:, d0 + r - s0:d0 + r - s0 + PACK_ROWS] = wt_ref[r:r + PACK_ROWS, :].T.astype(BF16)
    kb = wt_ref.shape[1]
    gates = jnp.concatenate(
        [wt_ref[_IN_OFFS[4]:_IN_OFFS[6], :], wt_ref[_IN_OFFS[8]:_IN_OFFS[9], :],
         jnp.zeros((GATE_W - GEND, kb), F32)], axis=0)
    o_ref[:, BIG_W:BIG_W + GATE_W] = gates.T.astype(BF16)


def _pack_w_in(wt):
    kb = LANE
    return pl.pallas_call(
        _pack_kernel,
        grid=(D_MODEL // kb,),
        in_specs=[pl.BlockSpec((IN_WIDTH, kb), lambda i: (0, i))],
        out_specs=pl.BlockSpec((kb, BIG_W + GATE_W), lambda i: (i, 0)),
        out_shape=jax.ShapeDtypeStruct((D_MODEL, BIG_W + GATE_W), BF16),
        compiler_params=pltpu.CompilerParams(
            dimension_semantics=("arbitrary",), vmem_limit_bytes=VMEM_LIMIT),
        name="pack_w_in",
    )(wt)


def _inproj_kernel(x_ref, g_ref, w_ref, meta_ref, cw_ref, cb_ref,
                   big_ref, gate_ref, bigm_ref, gatem_ref, u_ref, cs_ref, xh0_ref):
    tm = x_ref.shape[0]

    def normed(rows):
        ms = jnp.mean(rows * rows, axis=-1, keepdims=True)
        return (rows * lax.rsqrt(ms + EPS) * g_ref[...]).astype(BF16)

    @pl.when((pl.program_id(0) == 0) & (pl.program_id(1) == 0))
    def _():
        um = normed(meta_ref[...])
        bigm_ref[...] = jnp.zeros_like(bigm_ref)
        for c0 in list(range(K0, O0, NB)) + list(range(X0, GA0, NB)):
            acc = jnp.dot(um, w_ref[:, c0:c0 + NB], preferred_element_type=F32)
            bigm_ref[:, c0:c0 + NB] = acc.astype(BF16)
            if c0 >= X0:
                xh0_ref[:, c0 - X0:c0 - X0 + NB] = acc[CHUNK - SUBLANE:CHUNK, :]
        gatem_ref[...] = jnp.dot(um, w_ref[:, BIG_W:BIG_W + GATE_W], preferred_element_type=F32)

    @pl.when(pl.program_id(1) == 0)
    def _():
        for s in range(S_XBC // LANE):
            cs_ref[s, 0:SUBLANE, :] = xh0_ref[:, s * LANE:(s + 1) * LANE]

    u_ref[...] = normed(x_ref[...])
    for n in range(BIG_W // NB):
        c0 = n * NB
        acc = jnp.dot(u_ref[...], w_ref[:, c0:c0 + NB], preferred_element_type=F32)
        if c0 < K0:
            acc = acc * (M_DQK ** -0.5)
        elif O0 <= c0 < Z0 or c0 >= GA0:
            acc = jax.nn.sigmoid(acc)
        elif Z0 <= c0 < X0:
            acc = acc * jax.nn.sigmoid(acc)
        elif X0 <= c0 < GA0:
            for s in range(NB // LANE):
                slab = (c0 - X0) // LANE + s
                cols = slice((c0 - X0) + s * LANE, (c0 - X0) + (s + 1) * LANE)
                cs_ref[slab, SUBLANE:SUBLANE + tm, :] = acc[:, s * LANE:(s + 1) * LANE]
                half = 0.5 * cb_ref[:, cols]
                for tap in range(S_CONV):
                    back = S_CONV - 1 - tap
                    half = half + (cs_ref[slab, pl.ds(SUBLANE - back, tm, stride=1), :]
                                   * (0.5 * cw_ref[tap:tap + 1, cols]))
                cs_ref[slab, 0:SUBLANE, :] = cs_ref[slab, tm:tm + SUBLANE, :]
                big_ref[:, c0 + s * LANE:c0 + (s + 1) * LANE] = (half + half * jnp.tanh(half)).astype(BF16)
            continue
        big_ref[:, c0:c0 + NB] = acc.astype(BF16)
    gate_ref[...] = jnp.dot(u_ref[...], w_ref[:, BIG_W:BIG_W + GATE_W], preferred_element_type=F32)


def _inproj(rows, g, w_all, meta_rows, conv_w, conv_b, batch):
    n = rows.shape[0]
    tm = TM
    tiles = n // batch // tm
    row_map = lambda b, j: (b * tiles + j, 0)
    consts = (g, w_all, meta_rows, conv_w, conv_b)
    return pl.pallas_call(
        _inproj_kernel,
        grid=(batch, tiles),
        in_specs=[pl.BlockSpec((tm, D_MODEL), row_map)] + [_const_spec(a.shape) for a in consts],
        out_specs=[
            pl.BlockSpec((tm, BIG_W), row_map),
            pl.BlockSpec((tm, GATE_W), row_map),
            _const_spec((CHUNK, BIG_W)),
            _const_spec((CHUNK, GATE_W)),
        ],
        out_shape=[
            jax.ShapeDtypeStruct((n, BIG_W), BF16),
            jax.ShapeDtypeStruct((n, GATE_W), F32),
            jax.ShapeDtypeStruct((CHUNK, BIG_W), BF16),
            jax.ShapeDtypeStruct((CHUNK, GATE_W), F32),
        ],
        scratch_shapes=[
            pltpu.VMEM((tm, D_MODEL), BF16),
            pltpu.VMEM((S_XBC // LANE, SUBLANE + tm, LANE), F32),
            pltpu.VMEM((SUBLANE, S_XBC), F32),
        ],
        compiler_params=pltpu.CompilerParams(
            dimension_semantics=("arbitrary", "arbitrary"), vmem_limit_bytes=VMEM_LIMIT),
        name="inproj",
    )(rows, *consts)


def _gate_block(pre, alog_row, masked, tril3):
    L = pre.shape[0]
    lane = lax.broadcasted_iota(jnp.int32, (L, GATE_W), 1)
    sc = GATE_CAP * jnp.tanh(pre / GATE_CAP)
    i_log = sc * LOG2E
    f_log = -_softplus(-sc) * LOG2E
    dt = _softplus(pre)
    if masked:
        valid = lax.broadcasted_iota(jnp.int32, (L, GATE_W), 0) >= (L - N_META)
        i_log = jnp.where(valid, i_log, -jnp.inf)
        f_log = jnp.where(valid, f_log, 0.0)
        dt = jnp.where(valid, dt, 0.0)
    act = jnp.where(lane < GF0, i_log, jnp.where(lane < GDT0, f_log, jnp.where(lane < GEND, dt, 0.0)))
    a_row = -jnp.exp(alog_row) * LOG2E
    cs = jnp.where(lane < GF0, 0.0, jnp.where(lane < GDT0, act, jnp.where(lane < GEND, act * a_row, 0.0)))
    hi = cs.astype(BF16)
    rem = cs - hi.astype(F32)
    mid = rem.astype(BF16)
    lo = (rem - mid.astype(F32)).astype(BF16)
    cum = jnp.dot(tril3, jnp.concatenate([hi, mid, lo], axis=0), preferred_element_type=F32)
    return act, cum


def _tril3(L):
    assert L & (L - 1) == 0
    row_i = lax.broadcasted_iota(jnp.int32, (L, 3 * L), 0)
    col_i = lax.broadcasted_iota(jnp.int32, (L, 3 * L), 1)
    return (jnp.bitwise_and(col_i, L - 1) <= row_i).astype(F32).astype(BF16)


def _lane_rep(x, col):
    return jnp.broadcast_to(x[:, col:col + 1], (x.shape[0], LANE))


def _mlstm_state(k, v, bt_b, it_b, b_tot, m_prev, ct_ref, n_ref, m_ref, h):
    w_end = b_tot - bt_b + it_b
    m_loc = jnp.max(w_end, axis=0, keepdims=True)
    wgt = jnp.exp2(w_end - m_loc)
    kw = k.astype(F32) * wgt
    s_loc = lax.dot_general(kw.astype(BF16), v, (((0,), (0,)), ((), ())), preferred_element_type=F32)
    n_loc = jnp.sum(kw, axis=0, keepdims=True)
    m_new = jnp.maximum(b_tot + m_prev, m_loc)
    a_dec = jnp.exp2(b_tot + m_prev - m_new)
    s_dec = jnp.exp2(m_loc - m_new)
    ct_ref[h] = (jnp.concatenate([a_dec, a_dec], axis=1) * ct_ref[h]
                 + jnp.concatenate([s_dec, s_dec], axis=1) * s_loc)
    n_ref[h:h + 1, :] = a_dec * n_ref[h:h + 1, :] + s_dec * n_loc
    m_ref[h:h + 1, :] = m_new


def _head_expand(act, cum, expand_ref):
    L = act.shape[0]
    lane = lax.broadcasted_iota(jnp.int32, (L, GATE_W), 1)
    is_dt = (lane >= GDT0) & (lane < GEND)
    tot = cum[L - 1:L, :]
    e_cum = jnp.where(is_dt, jnp.exp2(cum), 0.0)
    e_end = jnp.where(is_dt, jnp.exp2(tot - cum) * act, 0.0)

    def split(x):
        hi = x.astype(BF16)
        lo = (x - hi.astype(F32)).astype(BF16)
        return jnp.concatenate([hi, lo], axis=1)

    lhs = jnp.concatenate([split(e_cum), split(e_end)], axis=0)
    return jnp.dot(lhs, expand_ref[...], preferred_element_type=F32)


def _chunk_step(r0, big_ref, gs, p, ct_ref, n_ref, m_ref, st_ref, hm_ref, ys_ref, hmask):
    L = CHUNK
    rows = pl.ds(r0, L)
    row_i = lax.broadcasted_iota(jnp.int32, (L, L), 0)
    col_i = lax.broadcasted_iota(jnp.int32, (L, L), 1)
    causal = col_i <= row_i
    neg_inf = -jnp.inf

    rep_ref = gs["rep"]
    act_t = gs["act_t"][rows, :]
    cum_t = gs["cum_t"][rows, :]
    e_rows = pl.multiple_of(2 * r0, 2 * L)

    zero_blk = jnp.zeros((L, LANE), BF16)

    def paired_nt(lhs_pair, rhs0, rhs1):
        rhs = jnp.concatenate([jnp.concatenate([rhs0, zero_blk], axis=1),
                               jnp.concatenate([zero_blk, rhs1], axis=1)], axis=0)
        out = lax.dot_general(lhs_pair, rhs, (((1,), (1,)), ((), ())), preferred_element_type=F32)
        return out[:, :LANE], out[:, LANE:]

    qk_heads = []
    for pr in range(M_HEADS // 2):
        qk_heads += paired_nt(
            big_ref[rows, Q0 + 2 * pr * M_DQK:Q0 + 2 * (pr + 1) * M_DQK],
            big_ref[rows, K0 + 2 * pr * M_DQK:K0 + (2 * pr + 1) * M_DQK],
            big_ref[rows, K0 + (2 * pr + 1) * M_DQK:K0 + 2 * (pr + 1) * M_DQK])

    def mlstm_head(h):
        q = big_ref[rows, Q0 + h * M_DQK:Q0 + (h + 1) * M_DQK]
        k = big_ref[rows, K0 + h * M_DQK:K0 + (h + 1) * M_DQK]
        v = big_ref[rows, V0 + h * M_DV:V0 + (h + 1) * M_DV]
        gi, gf = GI0 + h, GF0 + h
        bt_b = rep_ref[h, rows, :]
        it_b = rep_ref[M_HEADS + h, rows, :]
        bt_r = cum_t[gf:gf + 1, :]
        it_r = act_t[gi:gi + 1, :]
        b_tot = bt_b[L - 1:L, :]
        m_prev = m_ref[h:h + 1, :]
        qk = qk_heads[h]
        d_log = jnp.where(causal, bt_b - bt_r + it_r, neg_inf)
        mx = jnp.max(d_log, axis=1, keepdims=True)
        inter_log = bt_b + m_prev
        m_t = jnp.maximum(inter_log, mx)
        w_ts = jnp.exp2(d_log - m_t) * qk
        inter = jnp.exp2(inter_log - m_t)
        q_f = q.astype(F32)
        lhs = jnp.concatenate([w_ts.astype(BF16), (q_f * inter).astype(BF16)], axis=1)
        rhs = jnp.concatenate([v, ct_ref[h].astype(BF16)], axis=0)
        num = jnp.dot(lhs, rhs, preferred_element_type=F32)
        q_n = jnp.sum(q_f * n_ref[h:h + 1, :], axis=1, keepdims=True)
        den = jnp.sum(w_ts, axis=1, keepdims=True) + inter * q_n
        denom = jnp.maximum(jnp.abs(den), jnp.exp2(-m_t))
        ms = jnp.mean(num * num, axis=1, keepdims=True)
        rs = lax.rsqrt(ms + EPS * (denom * denom))
        y = num * jnp.concatenate([rs, rs], axis=1) * p["mng"][:, h * M_DV:(h + 1) * M_DV]
        o_gate = big_ref[rows, O0 + h * M_DV:O0 + (h + 1) * M_DV]
        hm_ref[rows, h * M_DV:(h + 1) * M_DV] = y.astype(BF16) * o_gate
        _mlstm_state(k, v, bt_b, it_b, b_tot, m_prev, ct_ref, n_ref, m_ref, h)

    e_ref = gs["expanded"]
    cb_groups = []
    for pr in range(S_GROUPS // 2):
        cb_groups += paired_nt(
            big_ref[rows, X0 + XC0 + 2 * pr * S_STATE:X0 + XC0 + 2 * (pr + 1) * S_STATE],
            big_ref[rows, X0 + XB0 + 2 * pr * S_STATE:X0 + XB0 + (2 * pr + 1) * S_STATE],
            big_ref[rows, X0 + XB0 + (2 * pr + 1) * S_STATE:X0 + XB0 + 2 * (pr + 1) * S_STATE])

    def ssd_group(g):
        gcols = slice(g * S_GW, (g + 1) * S_GW)
        bg = big_ref[rows, X0 + XB0 + g * S_STATE:X0 + XB0 + (g + 1) * S_STATE]
        cg = big_ref[rows, X0 + XC0 + g * S_STATE:X0 + XC0 + (g + 1) * S_STATE]
        xg_b = big_ref[rows, X0 + g * S_GW:X0 + (g + 1) * S_GW]
        xg = xg_b.astype(F32)
        cbm = cb_groups[g]
        w_heads = []
        for j in range(S_HPG):
            gk = GDT0 + g * S_HPG + j
            ca_b = rep_ref[2 * M_HEADS + g * S_HPG + j, rows, :]
            ca_r = cum_t[gk:gk + 1, :]
            dt_r = act_t[gk:gk + 1, :]
            dec = jnp.exp2(jnp.where(causal, ca_b - ca_r, neg_inf))
            w_heads.append((cbm * dec * dt_r).astype(BF16))
        y_diag = jnp.zeros((L, S_GW), F32)
        for j in range(0, S_HPG, 2):
            y_diag = y_diag + jnp.dot(
                jnp.concatenate([w_heads[j], w_heads[j + 1]], axis=1),
                jnp.concatenate([xg_b * hmask[j], xg_b * hmask[j + 1]], axis=0),
                preferred_element_type=F32)
        e_cum = e_ref[pl.ds(e_rows, L), gcols]
        e_end = e_ref[pl.ds(e_rows + L, L), gcols]
        e_tot = e_ref[pl.ds(e_rows + L - SUBLANE, SUBLANE), gcols][SUBLANE - 1:SUBLANE, :]
        y_off = jnp.dot(cg, st_ref[g].astype(BF16), preferred_element_type=F32) * e_cum
        y = y_diag + y_off + p["sd"][:, gcols] * xg
        yz = y * big_ref[rows, Z0 + g * S_GW:Z0 + (g + 1) * S_GW].astype(F32)
        ms = jnp.mean(yz * yz, axis=1, keepdims=True)
        ys_ref[rows, gcols] = (yz * lax.rsqrt(ms + EPS) * p["sng"][:, gcols]).astype(BF16)
        xw = (xg * e_end).astype(BF16)
        s_new = lax.dot_general(bg, xw, (((0,), (0,)), ((), ())), preferred_element_type=F32)
        st_ref[g] = st_ref[g] * e_tot + s_new

    assert M_HEADS == S_GROUPS
    for i in range(M_HEADS):
        mlstm_head(i)
        ssd_group(i)


_SMALL_KEYS = ("gbias", "alog", "mng", "sd", "sng", "expand")


def _init_kernel(big_ref, gate_ref, gbias_ref, alog_ref, cw_ref, cb_ref, expand_ref,
                 ct_ref, n_ref, m_ref, st_ref):
    L = CHUNK
    act, cum = _gate_block(gate_ref[...] + gbias_ref[...], alog_ref[...], True, _tril3(L))
    ct_ref[...] = jnp.zeros_like(ct_ref)
    n_ref[...] = jnp.zeros_like(n_ref)
    m_ref[...] = jnp.zeros_like(m_ref)
    for h in range(M_HEADS):
        k = big_ref[:, K0 + h * M_DQK:K0 + (h + 1) * M_DQK]
        v = big_ref[:, V0 + h * M_DV:V0 + (h + 1) * M_DV]
        gi, gf = GI0 + h, GF0 + h
        bt_b = _lane_rep(cum, gf)
        _mlstm_state(k, v, bt_b, _lane_rep(act, gi), bt_b[L - 1:L, :], m_ref[h:h + 1, :],
                     ct_ref, n_ref, m_ref, h)

    e_all = _head_expand(act, cum, expand_ref)
    valid = lax.broadcasted_iota(jnp.int32, (L, S_XBC), 0) >= (L - N_META)
    xa = jnp.where(valid, big_ref[:, X0:X0 + S_XBC].astype(F32), 0.0)
    xc = _conv_silu(xa, jnp.zeros((SUBLANE, S_XBC), F32), cw_ref, cb_ref, slice(0, S_XBC))
    for g in range(S_GROUPS):
        gcols = slice(g * S_GW, (g + 1) * S_GW)
        bg = xc[:, XB0 + g * S_STATE:XB0 + (g + 1) * S_STATE].astype(BF16)
        xw = (xc[:, gcols] * e_all[L:2 * L, gcols]).astype(BF16)
        st_ref[g] = lax.dot_general(bg, xw, (((0,), (0,)), ((), ())), preferred_element_type=F32)


_STATE_SHAPES = (
    (M_HEADS, M_DQK, M_DV),
    (SUBLANE, M_DQK),
    (SUBLANE, LANE),
    (S_GROUPS, S_STATE, S_GW),
)


def _init_state(big_m, gate_m, consts):
    return pl.pallas_call(
        _init_kernel,
        grid=(1,),
        in_specs=[_const_spec(big_m.shape), _const_spec(gate_m.shape)] + [_const_spec(a.shape) for a in consts],
        out_specs=[_const_spec(s) for s in _STATE_SHAPES],
        out_shape=[jax.ShapeDtypeStruct(s, F32) for s in _STATE_SHAPES],
        compiler_params=pltpu.CompilerParams(
            dimension_semantics=("arbitrary",), vmem_limit_bytes=VMEM_LIMIT),
        name="init_state",
    )(big_m, gate_m, *consts)


def _mixer_kernel(big_ref, gate_ref, x_ref, *refs):
    ns = len(_SMALL_KEYS)
    p = dict(zip(_SMALL_KEYS, refs[:ns]))
    ct0, n0, m0, st0 = refs[ns:ns + 4]
    mproj_ref, sproj_ref, wout_ref = refs[ns + 4:ns + 7]
    h1_ref = refs[ns + 7]
    ct_ref, n_ref, m_ref, st_ref, hm_ref, ys_ref, mg_ref = refs[ns + 8:ns + 15]
    gs = dict(zip(("rep", "act_t", "cum_t", "expanded"), refs[ns + 15:]))

    @pl.when(pl.program_id(1) == 0)
    def _():
        ct_ref[...] = ct0[...]
        n_ref[...] = n0[...]
        m_ref[...] = m0[...]
        st_ref[...] = st0[...]

    tril3 = _tril3(CHUNK)

    def gate_prologue(c):
        rs = slice(c * CHUNK, (c + 1) * CHUNK)
        act, cum = _gate_block(gate_ref[rs, :] + p["gbias"][...], p["alog"][...], False, tril3)
        for h in range(M_HEADS):
            gs["rep"][h, rs, :] = _lane_rep(cum, GF0 + h)
            gs["rep"][M_HEADS + h, rs, :] = _lane_rep(act, GI0 + h)
        for j in range(S_HEADS):
            gs["rep"][2 * M_HEADS + j, rs, :] = _lane_rep(cum, GDT0 + j)
        gs["act_t"][rs, :] = act.T
        gs["cum_t"][rs, :] = cum.T
        gs["expanded"][2 * c * CHUNK:2 * (c + 1) * CHUNK, :] = _head_expand(act, cum, p["expand"])

    for c in range(PRO_AHEAD):
        gate_prologue(c)

    lane_g = lax.broadcasted_iota(jnp.int32, (CHUNK, S_GW), 1)
    hmask = [((lane_g >= j * S_HEADDIM) & (lane_g < (j + 1) * S_HEADDIM)).astype(F32).astype(BF16)
             for j in range(S_HPG)]

    def body(c, carry):
        r0 = pl.multiple_of(c * CHUNK, CHUNK)
        _chunk_step(r0, big_ref, gs, p, ct_ref, n_ref, m_ref, st_ref, hm_ref, ys_ref, hmask)
        return carry

    def merge_branches(rs):
        for nb in range(D_MODEL // FB):
            cols = slice(nb * FB, (nb + 1) * FB)
            br_a = jnp.dot(hm_ref[rs, :], mproj_ref[:, cols], preferred_element_type=F32)
            br_b = jnp.dot(ys_ref[rs, :], sproj_ref[:, cols], preferred_element_type=F32)
            ga = big_ref[rs, GA0 + nb * FB:GA0 + (nb + 1) * FB].astype(F32)
            gb = big_ref[rs, GB0 + nb * FB:GB0 + (nb + 1) * FB].astype(F32)
            mg_ref[rs, cols] = (ga * br_a + gb * br_b).astype(BF16)

    def project_out(rs):
        for nb in range(D_MODEL // FB):
            cols = slice(nb * FB, (nb + 1) * FB)
            h1_ref[rs, cols] = x_ref[rs, cols] + jnp.dot(mg_ref[rs, :], wout_ref[:, cols], preferred_element_type=F32)

    groups = [slice(c * PROJ_CHUNKS * CHUNK, (c + 1) * PROJ_CHUNKS * CHUNK) for c in range(CPT // PROJ_CHUNKS)]
    pending = []
    for c in range(CPT):
        body(c, 0)
        if c + PRO_AHEAD < CPT:
            gate_prologue(c + PRO_AHEAD)
        if pending:
            pending.pop(0)()
        if (c + 1) % PROJ_CHUNKS == 0:
            rs = groups[c // PROJ_CHUNKS]
            merge_branches(rs)
            pending.append(functools.partial(project_out, rs))
    for fn in pending:
        fn()


def _mixer(big, gate, x2, small, init, mproj, sproj, wout, batch):
    n = x2.shape[0]
    tm = MIX_TM
    tiles = n // batch // tm
    row_map = lambda b, j: (b * tiles + j, 0)
    state_shapes = _STATE_SHAPES
    assert PRO_AHEAD <= CPT
    return pl.pallas_call(
        _mixer_kernel,
        grid=(batch, tiles),
        in_specs=[
            pl.BlockSpec((tm, BIG_W), row_map),
            pl.BlockSpec((tm, GATE_W), row_map),
            pl.BlockSpec((tm, D_MODEL), row_map),
        ] + [_const_spec(s.shape) for s in small]
          + [_const_spec(s) for s in state_shapes]
          + [_const_spec((D_MODEL, D_MODEL))] * 3,
        out_specs=pl.BlockSpec((tm, D_MODEL), row_map),
        out_shape=jax.ShapeDtypeStruct((n, D_MODEL), F32),
        scratch_shapes=[pltpu.VMEM(s, F32) for s in state_shapes] + [
            pltpu.VMEM((tm, M_V), BF16),
            pltpu.VMEM((tm, S_INNER), BF16),
            pltpu.VMEM((tm, D_MODEL), BF16),
            pltpu.VMEM((2 * M_HEADS + S_HEADS, tm, LANE), F32),
            pltpu.VMEM((tm, GATE_W), F32),
            pltpu.VMEM((tm, GATE_W), F32),
            pltpu.VMEM((2 * tm, S_INNER), F32),
        ],
        compiler_params=pltpu.CompilerParams(
            dimension_semantics=("arbitrary", "arbitrary"), vmem_limit_bytes=VMEM_LIMIT),
        name="mixer",
    )(big, gate, x2, *small, *init, mproj, sproj, wout)


def _ffn_kernel(h_ref, g2_ref, gf_ref, w1_ref, w2_ref, o_ref, u_ref, hid_ref):
    h = h_ref[...]
    ms = jnp.mean(h * h, axis=-1, keepdims=True)
    u_ref[...] = (h * lax.rsqrt(ms + EPS) * g2_ref[...]).astype(BF16)
    for jb in range(D_FF // FB):
        gate = jnp.dot(u_ref[...], w1_ref[:, jb * FB:(jb + 1) * FB], preferred_element_type=F32)
        up = jnp.dot(u_ref[...], w1_ref[:, D_FF + jb * FB:D_FF + (jb + 1) * FB], preferred_element_type=F32)
        hid_ref[:, jb * FB:(jb + 1) * FB] = (gate * jax.nn.sigmoid(gate) * up).astype(BF16)
    for nb in range(D_MODEL // FB):
        cols = slice(nb * FB, (nb + 1) * FB)
        o_ref[:, cols] = h_ref[:, cols] + jnp.dot(hid_ref[...], w2_ref[:, cols], preferred_element_type=F32)
    h2 = o_ref[...]
    ms2 = jnp.mean(h2 * h2, axis=-1, keepdims=True)
    o_ref[...] = h2 * lax.rsqrt(ms2 + EPS) * gf_ref[...]


def _ffn(h1, g2, gf, w1, w2):
    n = h1.shape[0]
    tm = FFN_TM
    return pl.pallas_call(
        _ffn_kernel,
        grid=(n // tm,),
        in_specs=[
            pl.BlockSpec((tm, D_MODEL), lambda i: (i, 0)),
            _const_spec((1, D_MODEL)),
            _const_spec((1, D_MODEL)),
            _const_spec((D_MODEL, 2 * D_FF)),
            _const_spec((D_FF, D_MODEL)),
        ],
        out_specs=pl.BlockSpec((tm, D_MODEL), lambda i: (i, 0)),
        out_shape=jax.ShapeDtypeStruct((n, D_MODEL), F32),
        scratch_shapes=[pltpu.VMEM((tm, D_MODEL), BF16), pltpu.VMEM((tm, D_FF), BF16)],
        compiler_params=pltpu.CompilerParams(
            dimension_semantics=("arbitrary",), vmem_limit_bytes=VMEM_LIMIT),
        name="ffn",
    )(h1, g2, gf, w1, w2)


def _pad_lanes(row, width):
    return jnp.pad(row, ((0, 0), (0, width - row.shape[1])))


def kernel(x, meta, norm1_g, w_in, m_igate_b, m_fgate_b, m_norm_g, m_proj, s_conv_w, s_conv_b,
           s_dt_bias, s_A_log, s_D, s_norm_g, s_proj, w_out, norm2_g, w_ffn_in, w_ffn_out, norm_f_g):
    bsz, seq, dm = x.shape
    assert dm == D_MODEL and seq % TM == 0 and w_in.shape[0] == 1
    w_all = _pack_w_in(w_in[0].T)

    gbias = _pad_lanes(jnp.concatenate([m_igate_b[0], m_fgate_b[0], s_dt_bias[0]])[None].astype(F32), GATE_W)
    alog = _pad_lanes(jnp.concatenate([jnp.zeros((GDT0,), F32), s_A_log[0].astype(F32)])[None], GATE_W)
    conv_w = jnp.pad(s_conv_w[0].astype(F32), ((0, SUBLANE - S_CONV), (0, 0)))
    conv_b = s_conv_b[0][None].astype(F32)
    head_of_lane = jnp.arange(S_INNER) // S_HEADDIM
    expand1 = (jnp.arange(GATE_W)[:, None] == GDT0 + head_of_lane[None, :]).astype(BF16)
    expand = jnp.concatenate([expand1, expand1], axis=0)
    small = (
        gbias, alog,
        m_norm_g[0].reshape(1, M_V).astype(F32),
        jnp.repeat(s_D[0].astype(F32), S_HEADDIM)[None],
        s_norm_g[0].reshape(1, S_INNER).astype(F32),
        expand,
    )
    g1 = norm1_g[0][None].astype(F32)

    meta_rows = jnp.concatenate([jnp.zeros((CHUNK - N_META, D_MODEL), x.dtype), meta.astype(x.dtype)], axis=0)
    x2 = x.reshape(bsz * seq, dm)
    big, gate, big_m, gate_m = _inproj(x2, g1, w_all, meta_rows, conv_w, conv_b, bsz)
    init = _init_state(big_m, gate_m, (gbias, alog, conv_w, conv_b, expand))
    h1 = _mixer(big, gate, x2, small, init, m_proj[0].astype(BF16), s_proj[0].astype(BF16),
                w_out[0].astype(BF16), bsz)
    out = _ffn(h1, norm2_g[0][None].astype(F32), norm_f_g[None].astype(F32),
               w_ffn_in[0].astype(BF16), w_ffn_out[0].astype(BF16))
    return out.reshape(bsz, seq, dm)
```

```python
import functools
import math

import jax
import jax.numpy as jnp
from jax import lax
from jax.experimental import pallas as pl
from jax.experimental.pallas import tpu as pltpu

F32 = jnp.float32
BF16 = jnp.bfloat16
LOG2E = math.log2(math.e)

D_MODEL = 1024
N_META = 16
CHUNK = 128
EPS = 1e-6

M_HEADS = 4
M_DQK = 128
M_DV = 256
M_QK = M_HEADS * M_DQK
M_V = M_HEADS * M_DV
GATE_CAP = 15.0

S_HEADDIM = 64
S_INNER = D_MODEL
S_HEADS = S_INNER // S_HEADDIM
S_GROUPS = 4
S_HPG = S_HEADS // S_GROUPS
S_STATE = 128
S_CONV = 4
S_GW = S_HPG * S_HEADDIM
S_XBC = S_INNER + 2 * S_GROUPS * S_STATE
XB0 = S_INNER
XC0 = S_INNER + S_GROUPS * S_STATE

D_FF = 2816

Q0 = 0
K0 = Q0 + M_QK
V0 = K0 + M_QK
O0 = V0 + M_V
Z0 = O0 + M_V
X0 = Z0 + S_INNER
GA0 = X0 + S_XBC
GB0 = GA0 + D_MODEL
BIG_W = GB0 + D_MODEL
GATE_W = 128
GI0, GF0, GDT0, GEND = 0, M_HEADS, 2 * M_HEADS, 2 * M_HEADS + S_HEADS

LANE = 128
SUBLANE = 8
TM = 512
MIX_TM = 512
FFN_TM = 1024
CPT = MIX_TM // CHUNK
PROJ_CHUNKS = 2
PRO_AHEAD = 2
NB = 512
FB = 256
VMEM_LIMIT = 56 * 1024 * 1024


def _softplus(x):
    return jnp.maximum(x, 0.0) + jnp.log1p(jnp.exp(-jnp.abs(x)))


def _const_spec(shape):
    nd = len(shape)
    return pl.BlockSpec(shape, lambda *_: (0,) * nd, pipeline_mode=pl.Buffered(1))


def _conv_silu(xa, hist, cw_ref, cb_ref, cols):
    row8 = lax.broadcasted_iota(jnp.int32, (SUBLANE, xa.shape[1]), 0)
    acc = cb_ref[:, cols] + xa * cw_ref[S_CONV - 1:S_CONV, cols]
    for sh in range(1, S_CONV):
        rolled = pltpu.roll(xa, sh, axis=0)
        top = jnp.where(row8 < sh, pltpu.roll(hist, sh, axis=0), rolled[0:SUBLANE])
        shifted = jnp.concatenate([top, rolled[SUBLANE:]], axis=0)
        acc = acc + shifted * cw_ref[S_CONV - 1 - sh:S_CONV - sh, cols]
    return acc * jax.nn.sigmoid(acc)


_IN_SIZES = (M_QK, M_QK, M_V, M_V, M_HEADS, M_HEADS, S_INNER, S_XBC, S_HEADS, D_MODEL, D_MODEL)
_IN_OFFS = tuple(sum(_IN_SIZES[:i]) for i in range(len(_IN_SIZES) + 1))
IN_WIDTH = _IN_OFFS[-1]
_PACK_SEGMENTS = (
    (_IN_OFFS[0], _IN_OFFS[4], Q0),
    (_IN_OFFS[6], _IN_OFFS[7], Z0),
    (_IN_OFFS[7], _IN_OFFS[8], X0),
    (_IN_OFFS[9], _IN_OFFS[10], GA0),
    (_IN_OFFS[10], _IN_OFFS[11], GB0),
)
PACK_ROWS = 512


def _pack_kernel(wt_ref, o_ref):
    for s0, s1, d0 in _PACK_SEGMENTS:
        for r in range(s0, s1, PACK_ROWS):
            o_ref[:, d0 + r - s0:d0 + r - s0 + PACK_ROWS] = wt_ref[r:r + PACK_ROWS, :].T.astype(BF16)
    kb = wt_ref.shape[1]
    gates = jnp.concatenate(
        [wt_ref[_IN_OFFS[4]:_IN_OFFS[6], :], wt_ref[_IN_OFFS[8]:_IN_OFFS[9], :],
         jnp.zeros((GATE_W - GEND, kb), F32)], axis=0)
    o_ref[:, BIG_W:BIG_W + GATE_W] = gates.T.astype(BF16)


def _pack_w_in(wt):
    kb = LANE
    return pl.pallas_call(
        _pack_kernel,
        grid=(D_MODEL // kb,),
        in_specs=[pl.BlockSpec((IN_WIDTH, kb), lambda i: (0, i))],
        out_specs=pl.BlockSpec((kb, BIG_W + GATE_W), lambda i: (i, 0)),
        out_shape=jax.ShapeDtypeStruct((D_MODEL, BIG_W + GATE_W), BF16),
        compiler_params=pltpu.CompilerParams(
            dimension_semantics=("arbitrary",), vmem_limit_bytes=VMEM_LIMIT),
        name="pack_w_in",
    )(wt)


def _inproj_kernel(x_ref, g_ref, w_ref, meta_ref, cw_ref, cb_ref,
                   big_ref, gate_ref, bigm_ref, gatem_ref, u_ref, cs_ref, xh0_ref):
    tm = x_ref.shape[0]

    def normed(rows):
        ms = jnp.mean(rows * rows, axis=-1, keepdims=True)
        return (rows * lax.rsqrt(ms + EPS) * g_ref[...]).astype(BF16)

    @pl.when((pl.program_id(0) == 0) & (pl.program_id(1) == 0))
    def _():
        um = normed(meta_ref[...])
        bigm_ref[...] = jnp.zeros_like(bigm_ref)
        for c0 in list(range(K0, O0, NB)) + list(range(X0, GA0, NB)):
            acc = jnp.dot(um, w_ref[:, c0:c0 + NB], preferred_element_type=F32)
            bigm_ref[:, c0:c0 + NB] = acc.astype(BF16)
            if c0 >= X0:
                xh0_ref[:, c0 - X0:c0 - X0 + NB] = acc[CHUNK - SUBLANE:CHUNK, :]
        gatem_ref[...] = jnp.dot(um, w_ref[:, BIG_W:BIG_W + GATE_W], preferred_element_type=F32)

    @pl.when(pl.program_id(1) == 0)
    def _():
        for s in range(S_XBC // LANE):
            cs_ref[s, 0:SUBLANE, :] = xh0_ref[:, s * LANE:(s + 1) * LANE]

    u_ref[...] = normed(x_ref[...])
    blocks = list(range(BIG_W // NB))
    conv_blocks = [n for n in blocks if X0 <= n * NB < GA0]
    plain_blocks = [n for n in blocks if n not in conv_blocks]
    order = []
    for n in conv_blocks:
        order += [n, plain_blocks.pop(0)]
    order += plain_blocks
    for n in order:
        c0 = n * NB
        acc = jnp.dot(u_ref[...], w_ref[:, c0:c0 + NB], preferred_element_type=F32)
        if c0 < K0:
            acc = acc * (M_DQK ** -0.5)
        elif O0 <= c0 < Z0 or c0 >= GA0:
            acc = jax.nn.sigmoid(acc)
        elif Z0 <= c0 < X0:
            acc = acc * jax.nn.sigmoid(acc)
        elif X0 <= c0 < GA0:
            for s in range(NB // LANE):
                slab = (c0 - X0) // LANE + s
                cols = slice((c0 - X0) + s * LANE, (c0 - X0) + (s + 1) * LANE)
                cs_ref[slab, SUBLANE:SUBLANE + tm, :] = acc[:, s * LANE:(s + 1) * LANE]
                half = 0.5 * cb_ref[:, cols]
                for tap in range(S_CONV):
                    back = S_CONV - 1 - tap
                    half = half + (cs_ref[slab, pl.ds(SUBLANE - back, tm, stride=1), :]
                                   * (0.5 * cw_ref[tap:tap + 1, cols]))
                cs_ref[slab, 0:SUBLANE, :] = cs_ref[slab, tm:tm + SUBLANE, :]
                big_ref[:, c0 + s * LANE:c0 + (s + 1) * LANE] = (half + half * jnp.tanh(half)).astype(BF16)
            continue
        big_ref[:, c0:c0 + NB] = acc.astype(BF16)
    gate_ref[...] = jnp.dot(u_ref[...], w_ref[:, BIG_W:BIG_W + GATE_W], preferred_element_type=F32)


def _inproj(rows, g, w_all, meta_rows, conv_w, conv_b, batch):
    n = rows.shape[0]
    tm = TM
    tiles = n // batch // tm
    row_map = lambda b, j: (b * tiles + j, 0)
    consts = (g, w_all, meta_rows, conv_w, conv_b)
    return pl.pallas_call(
        _inproj_kernel,
        grid=(batch, tiles),
        in_specs=[pl.BlockSpec((tm, D_MODEL), row_map)] + [_const_spec(a.shape) for a in consts],
        out_specs=[
            pl.BlockSpec((tm, BIG_W), row_map),
            pl.BlockSpec((tm, GATE_W), row_map),
            _const_spec((CHUNK, BIG_W)),
            _const_spec((CHUNK, GATE_W)),
        ],
        out_shape=[
            jax.ShapeDtypeStruct((n, BIG_W), BF16),
            jax.ShapeDtypeStruct((n, GATE_W), F32),
            jax.ShapeDtypeStruct((CHUNK, BIG_W), BF16),
            jax.ShapeDtypeStruct((CHUNK, GATE_W), F32),
        ],
        scratch_shapes=[
            pltpu.VMEM((tm, D_MODEL), BF16),
            pltpu.VMEM((S_XBC // LANE, SUBLANE + tm, LANE), F32),
            pltpu.VMEM((SUBLANE, S_XBC), F32),
        ],
        compiler_params=pltpu.CompilerParams(
            dimension_semantics=("arbitrary", "arbitrary"), vmem_limit_bytes=VMEM_LIMIT),
        name="inproj",
    )(rows, *consts)


def _gate_block(pre, alog_row, masked, tril3):
    L = pre.shape[0]
    lane = lax.broadcasted_iota(jnp.int32, (L, GATE_W), 1)
    sc = GATE_CAP * jnp.tanh(pre / GATE_CAP)
    i_log = sc * LOG2E
    f_log = -_softplus(-sc) * LOG2E
    dt = _softplus(pre)
    if masked:
        valid = lax.broadcasted_iota(jnp.int32, (L, GATE_W), 0) >= (L - N_META)
        i_log = jnp.where(valid, i_log, -jnp.inf)
        f_log = jnp.where(valid, f_log, 0.0)
        dt = jnp.where(valid, dt, 0.0)
    act = jnp.where(lane < GF0, i_log, jnp.where(lane < GDT0, f_log, jnp.where(lane < GEND, dt, 0.0)))
    a_row = -jnp.exp(alog_row) * LOG2E
    cs = jnp.where(lane < GF0, 0.0, jnp.where(lane < GDT0, act, jnp.where(lane < GEND, act * a_row, 0.0)))
    hi = cs.astype(BF16)
    rem = cs - hi.astype(F32)
    mid = rem.astype(BF16)
    lo = (rem - mid.astype(F32)).astype(BF16)
    cum = jnp.dot(tril3, jnp.concatenate([hi, mid, lo], axis=0), preferred_element_type=F32)
    return act, cum


def _tril3(L):
    assert L & (L - 1) == 0
    row_i = lax.broadcasted_iota(jnp.int32, (L, 3 * L), 0)
    col_i = lax.broadcasted_iota(jnp.int32, (L, 3 * L), 1)
    return (jnp.bitwise_and(col_i, L - 1) <= row_i).astype(F32).astype(BF16)


def _lane_rep(x, col):
    return jnp.broadcast_to(x[:, col:col + 1], (x.shape[0], LANE))


def _mlstm_state(k, v, bt_b, it_b, b_tot, m_prev, ct_ref, n_ref, m_ref, h):
    w_end = b_tot - bt_b + it_b
    m_loc = jnp.max(w_end, axis=0, keepdims=True)
    wgt = jnp.exp2(w_end - m_loc)
    kw = k.astype(F32) * wgt
    s_loc = lax.dot_general(kw.astype(BF16), v, (((0,), (0,)), ((), ())), preferred_element_type=F32)
    n_loc = jnp.sum(kw, axis=0, keepdims=True)
    m_new = jnp.maximum(b_tot + m_prev, m_loc)
    a_dec = jnp.exp2(b_tot + m_prev - m_new)
    s_dec = jnp.exp2(m_loc - m_new)
    ct_ref[h] = (jnp.concatenate([a_dec, a_dec], axis=1) * ct_ref[h]
                 + jnp.concatenate([s_dec, s_dec], axis=1) * s_loc)
    n_ref[h:h + 1, :] = a_dec * n_ref[h:h + 1, :] + s_dec * n_loc
    m_ref[h:h + 1, :] = m_new


def _head_expand(act, cum, expand_ref):
    L = act.shape[0]
    lane = lax.broadcasted_iota(jnp.int32, (L, GATE_W), 1)
    is_dt = (lane >= GDT0) & (lane < GEND)
    tot = cum[L - 1:L, :]
    e_cum = jnp.where(is_dt, jnp.exp2(cum), 0.0)
    e_end = jnp.where(is_dt, jnp.exp2(tot - cum) * act, 0.0)

    def split(x):
        hi = x.astype(BF16)
        lo = (x - hi.astype(F32)).astype(BF16)
        return jnp.concatenate([hi, lo], axis=1)

    lhs = jnp.concatenate([split(e_cum), split(e_end)], axis=0)
    return jnp.dot(lhs, expand_ref[...], preferred_element_type=F32)


def _chunk_step(r0, big_ref, gs, p, ct_ref, n_ref, m_ref, st_ref, hm_ref, ys_ref, hmask):
    L = CHUNK
    rows = pl.ds(r0, L)
    row_i = lax.broadcasted_iota(jnp.int32, (L, L), 0)
    col_i = lax.broadcasted_iota(jnp.int32, (L, L), 1)
    causal = col_i <= row_i
    neg_inf = -jnp.inf

    rep_ref = gs["rep"]
    act_t = gs["act_t"][rows, :]
    cum_t = gs["cum_t"][rows, :]
    e_rows = pl.multiple_of(2 * r0, 2 * L)

    zero_blk = jnp.zeros((L, LANE), BF16)

    def paired_nt(lhs_pair, rhs0, rhs1):
        rhs = jnp.concatenate([jnp.concatenate([rhs0, zero_blk], axis=1),
                               jnp.concatenate([zero_blk, rhs1], axis=1)], axis=0)
        out = lax.dot_general(lhs_pair, rhs, (((1,), (1,)), ((), ())), preferred_element_type=F32)
        return out[:, :LANE], out[:, LANE:]

    qk_heads = []
    for pr in range(M_HEADS // 2):
        qk_heads += paired_nt(
            big_ref[rows, Q0 + 2 * pr * M_DQK:Q0 + 2 * (pr + 1) * M_DQK],
            big_ref[rows, K0 + 2 * pr * M_DQK:K0 + (2 * pr + 1) * M_DQK],
            big_ref[rows, K0 + (2 * pr + 1) * M_DQK:K0 + 2 * (pr + 1) * M_DQK])

    def mlstm_head(h):
        q = big_ref[rows, Q0 + h * M_DQK:Q0 + (h + 1) * M_DQK]
        k = big_ref[rows, K0 + h * M_DQK:K0 + (h + 1) * M_DQK]
        v = big_ref[rows, V0 + h * M_DV:V0 + (h + 1) * M_DV]
        gi, gf = GI0 + h, GF0 + h
        bt_b = rep_ref[h, rows, :]
        it_b = rep_ref[M_HEADS + h, rows, :]
        bt_r = cum_t[gf:gf + 1, :]
        it_r = act_t[gi:gi + 1, :]
        b_tot = bt_b[L - 1:L, :]
        m_prev = m_ref[h:h + 1, :]
        qk = qk_heads[h]
        d_log = jnp.where(causal, bt_b - bt_r + it_r, neg_inf)
        mx = jnp.max(d_log, axis=1, keepdims=True)
        inter_log = bt_b + m_prev
        m_t = jnp.maximum(inter_log, mx)
        w_ts = jnp.exp2(d_log - m_t) * qk
        inter = jnp.exp2(inter_log - m_t)
        q_f = q.astype(F32)
        lhs = jnp.concatenate([w_ts.astype(BF16), (q_f * inter).astype(BF16)], axis=1)
        rhs = jnp.concatenate([v, ct_ref[h].astype(BF16)], axis=0)
        num = jnp.dot(lhs, rhs, preferred_element_type=F32)
        q_n = jnp.sum(q_f * n_ref[h:h + 1, :], axis=1, keepdims=True)
        den = jnp.sum(w_ts, axis=1, keepdims=True) + inter * q_n
        denom = jnp.maximum(jnp.abs(den), jnp.exp2(-m_t))
        ms = jnp.mean(num * num, axis=1, keepdims=True)
        rs = lax.rsqrt(ms + EPS * (denom * denom))
        y = num * jnp.concatenate([rs, rs], axis=1) * p["mng"][:, h * M_DV:(h + 1) * M_DV]
        o_gate = big_ref[rows, O0 + h * M_DV:O0 + (h + 1) * M_DV]
        hm_ref[rows, h * M_DV:(h + 1) * M_DV] = y.astype(BF16) * o_gate
        _mlstm_state(k, v, bt_b, it_b, b_tot, m_prev, ct_ref, n_ref, m_ref, h)

    e_ref = gs["expanded"]
    cb_groups = []
    for pr in range(S_GROUPS // 2):
        cb_groups += paired_nt(
            big_ref[rows, X0 + XC0 + 2 * pr * S_STATE:X0 + XC0 + 2 * (pr + 1) * S_STATE],
            big_ref[rows, X0 + XB0 + 2 * pr * S_STATE:X0 + XB0 + (2 * pr + 1) * S_STATE],
            big_ref[rows, X0 + XB0 + (2 * pr + 1) * S_STATE:X0 + XB0 + 2 * (pr + 1) * S_STATE])

    def ssd_group(g):
        gcols = slice(g * S_GW, (g + 1) * S_GW)
        bg = big_ref[rows, X0 + XB0 + g * S_STATE:X0 + XB0 + (g + 1) * S_STATE]
        cg = big_ref[rows, X0 + XC0 + g * S_STATE:X0 + XC0 + (g + 1) * S_STATE]
        xg_b = big_ref[rows, X0 + g * S_GW:X0 + (g + 1) * S_GW]
        xg = xg_b.astype(F32)
        cbm = cb_groups[g]
        w_heads = []
        for j in range(S_HPG):
            gk = GDT0 + g * S_HPG + j
            ca_b = rep_ref[2 * M_HEADS + g * S_HPG + j, rows, :]
            ca_r = cum_t[gk:gk + 1, :]
            dt_r = act_t[gk:gk + 1, :]
            dec = jnp.exp2(jnp.where(causal, ca_b - ca_r, neg_inf))
            w_heads.append((cbm * dec * dt_r).astype(BF16))
        y_diag = jnp.zeros((L, S_GW), F32)
        for j in range(0, S_HPG, 2):
            y_diag = y_diag + jnp.dot(
                jnp.concatenate([w_heads[j], w_heads[j + 1]], axis=1),
                jnp.concatenate([xg_b * hmask[j], xg_b * hmask[j + 1]], axis=0),
                preferred_element_type=F32)
        e_cum = e_ref[pl.ds(e_rows, L), gcols]
        e_end = e_ref[pl.ds(e_rows + L, L), gcols]
        e_tot = e_ref[pl.ds(e_rows + L - SUBLANE, SUBLANE), gcols][SUBLANE - 1:SUBLANE, :]
        y_off = jnp.dot(cg, st_ref[g].astype(BF16), preferred_element_type=F32) * e_cum
        y = y_diag + y_off + p["sd"][:, gcols] * xg
        yz = y * big_ref[rows, Z0 + g * S_GW:Z0 + (g + 1) * S_GW].astype(F32)
        ms = jnp.mean(yz * yz, axis=1, keepdims=True)
        ys_ref[rows, gcols] = (yz * lax.rsqrt(ms + EPS) * p["sng"][:, gcols]).astype(BF16)
        xw = (xg * e_end).astype(BF16)
        s_new = lax.dot_general(bg, xw, (((0,), (0,)), ((), ())), preferred_element_type=F32)
        st_ref[g] = st_ref[g] * e_tot + s_new

    assert M_HEADS == S_GROUPS
    for i in range(M_HEADS):
        mlstm_head(i)
        ssd_group(i)


_SMALL_KEYS = ("gbias", "alog", "mng", "sd", "sng", "expand")


def _init_kernel(big_ref, gate_ref, gbias_ref, alog_ref, cw_ref, cb_ref, expand_ref,
                 ct_ref, n_ref, m_ref, st_ref):
    L = CHUNK
    act, cum = _gate_block(gate_ref[...] + gbias_ref[...], alog_ref[...], True, _tril3(L))
    ct_ref[...] = jnp.zeros_like(ct_ref)
    n_ref[...] = jnp.zeros_like(n_ref)
    m_ref[...] = jnp.zeros_like(m_ref)
    for h in range(M_HEADS):
        k = big_ref[:, K0 + h * M_DQK:K0 + (h + 1) * M_DQK]
        v = big_ref[:, V0 + h * M_DV:V0 + (h + 1) * M_DV]
        gi, gf = GI0 + h, GF0 + h
        bt_b = _lane_rep(cum, gf)
        _mlstm_state(k, v, bt_b, _lane_rep(act, gi), bt_b[L - 1:L, :], m_ref[h:h + 1, :],
                     ct_ref, n_ref, m_ref, h)

    e_all = _head_expand(act, cum, expand_ref)
    valid = lax.broadcasted_iota(jnp.int32, (L, S_XBC), 0) >= (L - N_META)
    xa = jnp.where(valid, big_ref[:, X0:X0 + S_XBC].astype(F32), 0.0)
    xc = _conv_silu(xa, jnp.zeros((SUBLANE, S_XBC), F32), cw_ref, cb_ref, slice(0, S_XBC))
    for g in range(S_GROUPS):
        gcols = slice(g * S_GW, (g + 1) * S_GW)
        bg = xc[:, XB0 + g * S_STATE:XB0 + (g + 1) * S_STATE].astype(BF16)
        xw = (xc[:, gcols] * e_all[L:2 * L, gcols]).astype(BF16)
        st_ref[g] = lax.dot_general(bg, xw, (((0,), (0,)), ((), ())), preferred_element_type=F32)


_STATE_SHAPES = (
    (M_HEADS, M_DQK, M_DV),
    (SUBLANE, M_DQK),
    (SUBLANE, LANE),
    (S_GROUPS, S_STATE, S_GW),
)


def _init_state(big_m, gate_m, consts):
    return pl.pallas_call(
        _init_kernel,
        grid=(1,),
        in_specs=[_const_spec(big_m.shape), _const_spec(gate_m.shape)] + [_const_spec(a.shape) for a in consts],
        out_specs=[_const_spec(s) for s in _STATE_SHAPES],
        out_shape=[jax.ShapeDtypeStruct(s, F32) for s in _STATE_SHAPES],
        compiler_params=pltpu.CompilerParams(
            dimension_semantics=("arbitrary",), vmem_limit_bytes=VMEM_LIMIT),
        name="init_state",
    )(big_m, gate_m, *consts)


def _mixer_kernel(big_ref, gate_ref, x_ref, *refs):
    ns = len(_SMALL_KEYS)
    p = dict(zip(_SMALL_KEYS, refs[:ns]))
    ct0, n0, m0, st0 = refs[ns:ns + 4]
    mproj_ref, sproj_ref, wout_ref = refs[ns + 4:ns + 7]
    h1_ref = refs[ns + 7]
    ct_ref, n_ref, m_ref, st_ref, hm_ref, ys_ref, mg_ref = refs[ns + 8:ns + 15]
    gs = dict(zip(("rep", "act_t", "cum_t", "expanded"), refs[ns + 15:]))

    @pl.when(pl.program_id(1) == 0)
    def _():
        ct_ref[...] = ct0[...]
        n_ref[...] = n0[...]
        m_ref[...] = m0[...]
        st_ref[...] = st0[...]

    tril3 = _tril3(CHUNK)

    def gate_prologue(c):
        rs = slice(c * CHUNK, (c + 1) * CHUNK)
        act, cum = _gate_block(gate_ref[rs, :] + p["gbias"][...], p["alog"][...], False, tril3)
        for h in range(M_HEADS):
            gs["rep"][h, rs, :] = _lane_rep(cum, GF0 + h)
            gs["rep"][M_HEADS + h, rs, :] = _lane_rep(act, GI0 + h)
        for j in range(S_HEADS):
            gs["rep"][2 * M_HEADS + j, rs, :] = _lane_rep(cum, GDT0 + j)
        gs["act_t"][rs, :] = act.T
        gs["cum_t"][rs, :] = cum.T
        gs["expanded"][2 * c * CHUNK:2 * (c + 1) * CHUNK, :] = _head_expand(act, cum, p["expand"])

    for c in range(PRO_AHEAD):
        gate_prologue(c)

    lane_g = lax.broadcasted_iota(jnp.int32, (CHUNK, S_GW), 1)
    hmask = [((lane_g >= j * S_HEADDIM) & (lane_g < (j + 1) * S_HEADDIM)).astype(F32).astype(BF16)
             for j in range(S_HPG)]

    def body(c, carry):
        r0 = pl.multiple_of(c * CHUNK, CHUNK)
        _chunk_step(r0, big_ref, gs, p, ct_ref, n_ref, m_ref, st_ref, hm_ref, ys_ref, hmask)
        return carry

    def merge_branches(rs):
        for nb in range(D_MODEL // FB):
            cols = slice(nb * FB, (nb + 1) * FB)
            br_a = jnp.dot(hm_ref[rs, :], mproj_ref[:, cols], preferred_element_type=F32)
            br_b = jnp.dot(ys_ref[rs, :], sproj_ref[:, cols], preferred_element_type=F32)
            ga = big_ref[rs, GA0 + nb * FB:GA0 + (nb + 1) * FB].astype(F32)
            gb = big_ref[rs, GB0 + nb * FB:GB0 + (nb + 1) * FB].astype(F32)
            mg_ref[rs, cols] = (ga * br_a + gb * br_b).astype(BF16)

    def project_out(rs):
        for nb in range(D_MODEL // FB):
            cols = slice(nb * FB, (nb + 1) * FB)
            h1_ref[rs, cols] = x_ref[rs, cols] + jnp.dot(mg_ref[rs, :], wout_ref[:, cols], preferred_element_type=F32)

    groups = [slice(c * PROJ_CHUNKS * CHUNK, (c + 1) * PROJ_CHUNKS * CHUNK) for c in range(CPT // PROJ_CHUNKS)]
    pending = []
    for c in range(CPT):
        body(c, 0)
        if c + PRO_AHEAD < CPT:
            gate_prologue(c + PRO_AHEAD)
        if pending:
            pending.pop(0)()
        if (c + 1) % PROJ_CHUNKS == 0:
            rs = groups[c // PROJ_CHUNKS]
            merge_branches(rs)
            pending.append(functools.partial(project_out, rs))
    for fn in pending:
        fn()


def _mixer(big, gate, x2, small, init, mproj, sproj, wout, batch):
    n = x2.shape[0]
    tm = MIX_TM
    tiles = n // batch // tm
    row_map = lambda b, j: (b * tiles + j, 0)
    state_shapes = _STATE_SHAPES
    assert PRO_AHEAD <= CPT
    return pl.pallas_call(
        _mixer_kernel,
        grid=(batch, tiles),
        in_specs=[
            pl.BlockSpec((tm, BIG_W), row_map),
            pl.BlockSpec((tm, GATE_W), row_map),
            pl.BlockSpec((tm, D_MODEL), row_map),
        ] + [_const_spec(s.shape) for s in small]
          + [_const_spec(s) for s in state_shapes]
          + [_const_spec((D_MODEL, D_MODEL))] * 3,
        out_specs=pl.BlockSpec((tm, D_MODEL), row_map),
        out_shape=jax.ShapeDtypeStruct((n, D_MODEL), F32),
        scratch_shapes=[pltpu.VMEM(s, F32) for s in state_shapes] + [
            pltpu.VMEM((tm, M_V), BF16),
            pltpu.VMEM((tm, S_INNER), BF16),
            pltpu.VMEM((tm, D_MODEL), BF16),
            pltpu.VMEM((2 * M_HEADS + S_HEADS, tm, LANE), F32),
            pltpu.VMEM((tm, GATE_W), F32),
            pltpu.VMEM((tm, GATE_W), F32),
            pltpu.VMEM((2 * tm, S_INNER), F32),
        ],
        compiler_params=pltpu.CompilerParams(
            dimension_semantics=("arbitrary", "arbitrary"), vmem_limit_bytes=VMEM_LIMIT),
        name="mixer",
    )(big, gate, x2, *small, *init, mproj, sproj, wout)


def _ffn_kernel(h_ref, g2_ref, gf_ref, w1_ref, w2_ref, o_ref, u_ref, hid_ref):
    tm = h_ref.shape[0]

    def norm_in(rs):
        h = h_ref[rs, :]
        ms = jnp.mean(h * h, axis=-1, keepdims=True)
        u_ref[rs, :] = (h * lax.rsqrt(ms + EPS) * g2_ref[...]).astype(BF16)

    def hidden_block(rs, jb):
        gate = jnp.dot(u_ref[rs, :], w1_ref[:, jb * FB:(jb + 1) * FB], preferred_element_type=F32)
        up = jnp.dot(u_ref[rs, :], w1_ref[:, D_FF + jb * FB:D_FF + (jb + 1) * FB], preferred_element_type=F32)
        hid_ref[rs, jb * FB:(jb + 1) * FB] = (gate * jax.nn.sigmoid(gate) * up).astype(BF16)

    def out_block(rs, nb):
        cols = slice(nb * FB, (nb + 1) * FB)
        o_ref[rs, cols] = h_ref[rs, cols] + jnp.dot(hid_ref[rs, :], w2_ref[:, cols], preferred_element_type=F32)

    def norm_out(rs):
        h2 = o_ref[rs, :]
        ms2 = jnp.mean(h2 * h2, axis=-1, keepdims=True)
        o_ref[rs, :] = h2 * lax.rsqrt(ms2 + EPS) * gf_ref[...]

    first, second = slice(0, tm // 2), slice(tm // 2, tm)
    norm_in(first)
    for jb in range(D_FF // FB):
        hidden_block(first, jb)
        if jb == 0:
            norm_in(second)
    for nb in range(D_MODEL // FB):
        out_block(first, nb)
    for jb in range(D_FF // FB):
        hidden_block(second, jb)
        if jb == 0:
            norm_out(first)
    for nb in range(D_MODEL // FB):
        out_block(second, nb)
    norm_out(second)


def _ffn(h1, g2, gf, w1, w2):
    n = h1.shape[0]
    tm = FFN_TM
    return pl.pallas_call(
        _ffn_kernel,
        grid=(n // tm,),
        in_specs=[
            pl.BlockSpec((tm, D_MODEL), lambda i: (i, 0)),
            _const_spec((1, D_MODEL)),
            _const_spec((1, D_MODEL)),
            _const_spec((D_MODEL, 2 * D_FF)),
            _const_spec((D_FF, D_MODEL)),
        ],
        out_specs=pl.BlockSpec((tm, D_MODEL), lambda i: (i, 0)),
        out_shape=jax.ShapeDtypeStruct((n, D_MODEL), F32),
        scratch_shapes=[pltpu.VMEM((tm, D_MODEL), BF16), pltpu.VMEM((tm, D_FF), BF16)],
        compiler_params=pltpu.CompilerParams(
            dimension_semantics=("arbitrary",), vmem_limit_bytes=VMEM_LIMIT),
        name="ffn",
    )(h1, g2, gf, w1, w2)


def _pad_lanes(row, width):
    return jnp.pad(row, ((0, 0), (0, width - row.shape[1])))


def kernel(x, meta, norm1_g, w_in, m_igate_b, m_fgate_b, m_norm_g, m_proj, s_conv_w, s_conv_b,
           s_dt_bias, s_A_log, s_D, s_norm_g, s_proj, w_out, norm2_g, w_ffn_in, w_ffn_out, norm_f_g):
    bsz, seq, dm = x.shape
    assert dm == D_MODEL and seq % TM == 0 and w_in.shape[0] == 1
    w_all = _pack_w_in(w_in[0].T)

    gbias = _pad_lanes(jnp.concatenate([m_igate_b[0], m_fgate_b[0], s_dt_bias[0]])[None].astype(F32), GATE_W)
    alog = _pad_lanes(jnp.concatenate([jnp.zeros((GDT0,), F32), s_A_log[0].astype(F32)])[None], GATE_W)
    conv_w = jnp.pad(s_conv_w[0].astype(F32), ((0, SUBLANE - S_CONV), (0, 0)))
    conv_b = s_conv_b[0][None].astype(F32)
    head_of_lane = jnp.arange(S_INNER) // S_HEADDIM
    expand1 = (jnp.arange(GATE_W)[:, None] == GDT0 + head_of_lane[None, :]).astype(BF16)
    expand = jnp.concatenate([expand1, expand1], axis=0)
    small = (
        gbias, alog,
        m_norm_g[0].reshape(1, M_V).astype(F32),
        jnp.repeat(s_D[0].astype(F32), S_HEADDIM)[None],
        s_norm_g[0].reshape(1, S_INNER).astype(F32),
        expand,
    )
    g1 = norm1_g[0][None].astype(F32)

    meta_rows = jnp.concatenate([jnp.zeros((CHUNK - N_META, D_MODEL), x.dtype), meta.astype(x.dtype)], axis=0)
    x2 = x.reshape(bsz * seq, dm)
    big, gate, big_m, gate_m = _inproj(x2, g1, w_all, meta_rows, conv_w, conv_b, bsz)
    init = _init_state(big_m, gate_m, (gbias, alog, conv_w, conv_b, expand))
    h1 = _mixer(big, gate, x2, small, init, m_proj[0].astype(BF16), s_proj[0].astype(BF16),
                w_out[0].astype(BF16), bsz)
    out = _ffn(h1, norm2_g[0][None].astype(F32), norm_f_g[None].astype(F32),
               w_ffn_in[0].astype(BF16), w_ffn_out[0].astype(BF16))
    return out.reshape(bsz, seq, dm)
```
